```python
import math
import jax, jax.numpy as jnp
from jax import lax
import numpy as np

D_MODEL = 2048
BATCH = 8
SEQ = 2048
DEPTH = 4
DEC_BATCH = 4
DEC_SEQ = 8192
PAST_LEN = 128

N_MIXERS = 3
N_RG_LAYERS = (DEPTH + 2) // 3
N_SSD_LAYERS = (DEPTH + 1) // 3
N_NA_LAYERS = DEPTH // 3
ALPHA = (2 * DEPTH) ** 0.25
BETA = (8 * DEPTH) ** -0.25
LN_EPS = 1e-5
CONV_W = 4
CONV_PAD_L = 1
LRU_W = D_MODEL
LRU_BLOCKS = 8
LRU_BW = LRU_W // LRU_BLOCKS
LRU_C = 8.0
SSD_D_INNER = 2 * D_MODEL
SSD_HEAD_DIM = 64
SSD_HEADS = SSD_D_INNER // SSD_HEAD_DIM
SSD_GROUPS = 8
SSD_STATE = 128
SSD_CHUNK = 128
SSD_CONV_DIM = SSD_D_INNER + 2 * SSD_GROUPS * SSD_STATE
SSD_IN_DIM = SSD_D_INNER + SSD_CONV_DIM + 2 * SSD_HEADS
GRID_W = 64
WIN_H = 8
WIN_W = 16
NA_HEAD_DIM = 128
NA_HEADS = D_MODEL // NA_HEAD_DIM
NA_QBLK = WIN_W
NA_KBLK = 2 * WIN_W
NA_NBLK = GRID_W // NA_QBLK
MLP_HIDDEN = 4 * D_MODEL

kernel_name = "hybrid_bidir_rglru_ssd_natten_encoder"


def layer_norm(x, g, b):
    xf = x.astype(jnp.float32)
    mu = jnp.mean(xf, axis=-1, keepdims=True)
    xc = xf - mu
    var = jnp.mean(xc * xc, axis=-1, keepdims=True)
    return (xc * lax.rsqrt(var + LN_EPS) * g + b).astype(x.dtype)


def centred_dwconv(x, w, b):
    L = x.shape[1]
    xp = jnp.pad(x, ((0, 0), (CONV_PAD_L, CONV_W - 1 - CONV_PAD_L), (0, 0)))
    y = b
    for k in range(CONV_W):
        y = y + xp[:, k:k + L] * w[k]
    return y


def _lin_combine(left, right):
    a1, b1 = left
    a2, b2 = right
    return a1 * a2, a2 * b1 + b2


def linear_scan(a, b, reverse):
    if reverse:
        a, b = jnp.flip(a, axis=1), jnp.flip(b, axis=1)
    _, h = lax.associative_scan(_lin_combine, (a, b), axis=1)
    return jnp.flip(h, axis=1) if reverse else h


def rglru_mixer(x, w_in, conv_w, conv_b, w_a, b_a, w_x, b_x, lam, w_out):
    bsz, L, _ = x.shape
    gate, u = jnp.split(x @ w_in, 2, axis=-1)
    gate = jax.nn.gelu(gate)
    u = centred_dwconv(u, conv_w, conv_b)
    ub = u.reshape(bsz, L, LRU_BLOCKS, LRU_BW)
    h = jnp.zeros((bsz, L, LRU_W), jnp.float32)
    for d in range(2):
        r = jax.nn.sigmoid(jnp.einsum('blnk,nkj->blnj', ub, w_a[d]).reshape(bsz, L, LRU_W) + b_a[d])
        i = jax.nn.sigmoid(jnp.einsum('blnk,nkj->blnj', ub, w_x[d]).reshape(bsz, L, LRU_W) + b_x[d])
        log_a = -LRU_C * r.astype(jnp.float32) * jax.nn.softplus(-lam[d].astype(jnp.float32))
        a = jnp.exp(log_a)
        inp = jnp.sqrt(-jnp.expm1(2.0 * log_a)) * (i * u).astype(jnp.float32)
        h = h + linear_scan(a, inp, reverse=(d == 1))
    y = (h * gate.astype(jnp.float32)).astype(x.dtype)
    return y @ w_out


def ssd_chunked(xs, dt, A, B, C):
    bsz, L, H, P = xs.shape
    G, N = B.shape[-2:]
    Hg = H // G
    nc = L // SSD_CHUNK
    f32 = jnp.float32
    x = xs.astype(f32).reshape(bsz, nc, SSD_CHUNK, G, Hg, P)
    dt = dt.astype(f32).reshape(bsz, nc, SSD_CHUNK, G, Hg)
    B = B.astype(f32).reshape(bsz, nc, SSD_CHUNK, G, N)
    C = C.astype(f32).reshape(bsz, nc, SSD_CHUNK, G, N)
    xdt = x * dt[..., None]
    a_cum = jnp.cumsum(dt * A.astype(f32).reshape(G, Hg), axis=2)
    causal = jnp.tril(jnp.ones((SSD_CHUNK, SSD_CHUNK), dtype=bool))
    seg = a_cum[:, :, :, None] - a_cum[:, :, None, :]
    decay = jnp.exp(jnp.where(causal[:, :, None, None], seg, -jnp.inf))
    cb = jnp.einsum('bclgn,bcsgn->bclsg', C, B)
    y_diag = jnp.einsum('bclsgh,bcsghp->bclghp', cb[..., None] * decay, xdt)
    decay_end = jnp.exp(a_cum[:, :, -1:] - a_cum)
    states = jnp.einsum('bclgn,bclghp->bcghpn', B, xdt * decay_end[..., None])
    chunk_decay = jnp.exp(a_cum[:, :, -1])

    def carry_step(h, inp):
        st, dec = inp
        return h * dec[..., None, None] + st, h

    h0 = jnp.zeros((bsz, G, Hg, P, N), f32)
    _, h_in = lax.scan(carry_step, h0, (jnp.moveaxis(states, 1, 0), jnp.moveaxis(chunk_decay, 1, 0)))
    h_in = jnp.moveaxis(h_in, 0, 1)
    y_off = jnp.einsum('bclgn,bcghpn->bclghp', C, h_in) * jnp.exp(a_cum)[..., None]
    return (y_diag + y_off).reshape(bsz, L, H, P)


def ssd_mixer(x, w_in, conv_w, conv_b, dt_bias, a_log, d_skip, norm_g, w_out):
    bsz, L, _ = x.shape
    f32 = jnp.float32
    proj = x @ w_in
    z, xbc, dt = jnp.split(proj, [SSD_D_INNER, SSD_D_INNER + SSD_CONV_DIM], axis=-1)
    xbc = jax.nn.silu(centred_dwconv(xbc, conv_w, conv_b))
    xs, Bm, Cm = jnp.split(xbc, [SSD_D_INNER, SSD_D_INNER + SSD_GROUPS * SSD_STATE], axis=-1)
    xs = xs.reshape(bsz, L, SSD_HEADS, SSD_HEAD_DIM)
    Bm = Bm.reshape(bsz, L, SSD_GROUPS, SSD_STATE)
    Cm = Cm.reshape(bsz, L, SSD_GROUPS, SSD_STATE)
    dt = jax.nn.softplus(dt.astype(f32).reshape(bsz, L, 2, SSD_HEADS) + dt_bias.astype(f32))
    A = -jnp.exp(a_log.astype(f32))
    flip = lambda t: jnp.flip(t, axis=1)
    y_f = ssd_chunked(xs, dt[:, :, 0], A[0], Bm, Cm)
    y_b = flip(ssd_chunked(flip(xs), flip(dt[:, :, 1]), A[1], flip(Bm), flip(Cm)))
    y = y_f + y_b + xs.astype(f32) * d_skip.astype(f32)[:, None]
    y = y.reshape(bsz, L, SSD_D_INNER) * jax.nn.silu(z.astype(f32))
    yg = y.reshape(bsz, L, SSD_GROUPS, SSD_D_INNER // SSD_GROUPS)
    yg = yg * lax.rsqrt(jnp.mean(yg * yg, axis=-1, keepdims=True) + LN_EPS)
    y = (yg.reshape(bsz, L, SSD_D_INNER) * norm_g).astype(x.dtype)
    return y @ w_out


def natten_mixer(x, w_qkv, b_qkv, rpb, w_out):
    bsz, L, _ = x.shape
    rows = L // GRID_W
    kh = min(WIN_H, rows)
    qkv = (x @ w_qkv + b_qkv).reshape(bsz, rows, GRID_W, 3, NA_HEADS, NA_HEAD_DIM)
    q = qkv[:, :, :, 0] * (NA_HEAD_DIM ** -0.5)
    k = qkv[:, :, :, 1]
    v = qkv[:, :, :, 2]
    qcol = np.arange(GRID_W).reshape(NA_NBLK, NA_QBLK)
    kstart = np.clip(np.arange(NA_NBLK) * NA_QBLK - WIN_W // 2, 0, GRID_W - NA_KBLK)
    kcol = kstart[:, None] + np.arange(NA_KBLK)
    wstart = np.clip(qcol - WIN_W // 2, 0, GRID_W - WIN_W)
    col_ok = (kcol[:, None, :] >= wstart[..., None]) & (kcol[:, None, :] < wstart[..., None] + WIN_W)
    dx_idx = np.clip(kcol[:, None, :] - qcol[:, :, None] + WIN_W - 1, 0, 2 * WIN_W - 2)
    col_bias = jnp.transpose(rpb[:, :, dx_idx], (0, 2, 3, 1, 4)).astype(jnp.float32)
    kc = k[:, :, kcol]
    vc = v[:, :, kcol]
    q = q.reshape(bsz, rows, NA_NBLK, NA_QBLK, NA_HEADS, NA_HEAD_DIM)

    def row_block(r):
        rs = jnp.clip(r - kh // 2, 0, rows - kh)
        kr = lax.dynamic_slice_in_dim(kc, rs, kh, axis=1)
        vr = lax.dynamic_slice_in_dim(vc, rs, kh, axis=1)
        qr = lax.dynamic_index_in_dim(q, r, axis=1, keepdims=False)
        s = jnp.einsum('bjqhd,bkjchd->bhjqkc', qr, kr).astype(jnp.float32)
        dy_idx = rs + jnp.arange(kh) - r + (WIN_H - 1)
        s = s + jnp.take(col_bias, dy_idx, axis=3)[None]
        s = jnp.where(col_ok[:, :, None, :], s, -jnp.inf)
        p = jax.nn.softmax(s.reshape(bsz, NA_HEADS, NA_NBLK, NA_QBLK, kh * NA_KBLK), axis=-1)
        p = p.reshape(s.shape).astype(vr.dtype)
        return jnp.einsum('bhjqkc,bkjchd->bjqhd', p, vr)

    o = lax.map(row_block, jnp.arange(rows))
    o = jnp.moveaxis(o, 0, 1).reshape(bsz, L, D_MODEL)
    return o @ w_out


def sq_relu_mlp(x, w_up, w_down):
    return jnp.square(jax.nn.relu(x @ w_up)) @ w_down


def setup_inputs(seed: int = 0) -> dict:
    key = jax.random.key(seed)
    ks = iter(jax.random.split(key, 32))
    f32 = jnp.float32

    def nrm(shape, fan_in, scale=1.0):
        return jax.random.normal(next(ks), shape, f32) * (scale * fan_in ** -0.5)

    def small(shape, s=0.01):
        return jax.random.normal(next(ks), shape, f32) * s

    x_prompt = jax.random.normal(next(ks), (BATCH, SEQ, D_MODEL), f32)
    x_sample = jax.random.normal(next(ks), (DEC_BATCH, DEC_SEQ, D_MODEL), f32)
    nR, nS, nN = N_RG_LAYERS, N_SSD_LAYERS, N_NA_LAYERS
    rg_w_in = nrm((nR, D_MODEL, 2 * LRU_W), D_MODEL)
    rg_conv_w = nrm((nR, CONV_W, LRU_W), CONV_W)
    rg_conv_b = small((nR, LRU_W))
    rg_w_a = nrm((nR, 2, LRU_BLOCKS, LRU_BW, LRU_BW), LRU_BW)
    rg_b_a = small((nR, 2, LRU_W))
    rg_w_x = nrm((nR, 2, LRU_BLOCKS, LRU_BW, LRU_BW), LRU_BW)
    rg_b_x = small((nR, 2, LRU_W))
    a0 = jax.random.uniform(next(ks), (nR, 2, LRU_W), f32, minval=0.9, maxval=0.999)
    s0 = a0 ** (1.0 / LRU_C)
    rg_lambda = jnp.log(s0) - jnp.log1p(-s0)
    rg_w_out = nrm((nR, LRU_W, D_MODEL), LRU_W, BETA)
    ssd_w_in = nrm((nS, D_MODEL, SSD_IN_DIM), D_MODEL)
    ssd_conv_w = nrm((nS, CONV_W, SSD_CONV_DIM), CONV_W)
    ssd_conv_b = small((nS, SSD_CONV_DIM))
    dt0 = jnp.exp(jax.random.uniform(next(ks), (nS, 2, SSD_HEADS), f32, minval=math.log(1e-3), maxval=math.log(1e-1)))
    ssd_dt_bias = dt0 + jnp.log(-jnp.expm1(-dt0))
    ssd_a_log = jnp.log(jax.random.uniform(next(ks), (nS, 2, SSD_HEADS), f32, minval=1.0, maxval=16.0))
    ssd_d = 1.0 + small((nS, SSD_HEADS))
    ssd_norm_g = 1.0 + small((nS, SSD_D_INNER))
    ssd_w_out = nrm((nS, SSD_D_INNER, D_MODEL), SSD_D_INNER, BETA)
    na_w_qkv = nrm((nN, D_MODEL, 3 * D_MODEL), D_MODEL)
    na_b_qkv = small((nN, 3 * D_MODEL))
    na_rpb = small((nN, NA_HEADS, 2 * WIN_H - 1, 2 * WIN_W - 1), 0.02)
    na_w_out = nrm((nN, D_MODEL, D_MODEL), D_MODEL, BETA)
    mlp_w_up = nrm((DEPTH, D_MODEL, MLP_HIDDEN), D_MODEL)
    mlp_w_down = nrm((DEPTH, MLP_HIDDEN, D_MODEL), MLP_HIDDEN, BETA)
    ln1_g = 1.0 + small((DEPTH, D_MODEL))
    ln1_b = small((DEPTH, D_MODEL))
    ln2_g = 1.0 + small((DEPTH, D_MODEL))
    ln2_b = small((DEPTH, D_MODEL))
    return {"x_prompt": x_prompt, "x_sample": x_sample,
            "rg_w_in": rg_w_in, "rg_conv_w": rg_conv_w, "rg_conv_b": rg_conv_b,
            "rg_w_a": rg_w_a, "rg_b_a": rg_b_a, "rg_w_x": rg_w_x, "rg_b_x": rg_b_x,
            "rg_lambda": rg_lambda, "rg_w_out": rg_w_out,
            "ssd_w_in": ssd_w_in, "ssd_conv_w": ssd_conv_w, "ssd_conv_b": ssd_conv_b,
            "ssd_dt_bias": ssd_dt_bias, "ssd_a_log": ssd_a_log, "ssd_d": ssd_d,
            "ssd_norm_g": ssd_norm_g, "ssd_w_out": ssd_w_out,
            "na_w_qkv": na_w_qkv, "na_b_qkv": na_b_qkv, "na_rpb": na_rpb, "na_w_out": na_w_out,
            "mlp_w_up": mlp_w_up, "mlp_w_down": mlp_w_down,
            "ln1_g": ln1_g, "ln1_b": ln1_b, "ln2_g": ln2_g, "ln2_b": ln2_b}


def reference(x_prompt, x_sample,
              rg_w_in, rg_conv_w, rg_conv_b, rg_w_a, rg_b_a, rg_w_x, rg_b_x, rg_lambda, rg_w_out,
              ssd_w_in, ssd_conv_w, ssd_conv_b, ssd_dt_bias, ssd_a_log, ssd_d, ssd_norm_g, ssd_w_out,
              na_w_qkv, na_b_qkv, na_rpb, na_w_out,
              mlp_w_up, mlp_w_down,
              ln1_g, ln1_b, ln2_g, ln2_b):
    def trunk(x):
        for i in range(DEPTH):
            kind = i % N_MIXERS
            j = i // N_MIXERS
            if kind == 0:
                h = rglru_mixer(x, rg_w_in[j], rg_conv_w[j], rg_conv_b[j], rg_w_a[j], rg_b_a[j],
                                rg_w_x[j], rg_b_x[j], rg_lambda[j], rg_w_out[j])
            elif kind == 1:
                h = ssd_mixer(x, ssd_w_in[j], ssd_conv_w[j], ssd_conv_b[j], ssd_dt_bias[j],
                              ssd_a_log[j], ssd_d[j], ssd_norm_g[j], ssd_w_out[j])
            else:
                h = natten_mixer(x, na_w_qkv[j], na_b_qkv[j], na_rpb[j], na_w_out[j])
            x = layer_norm(ALPHA * x + h, ln1_g[i], ln1_b[i])
            x = layer_norm(ALPHA * x + sq_relu_mlp(x, mlp_w_up[i], mlp_w_down[i]), ln2_g[i], ln2_b[i])
        return x

    y_prompt = trunk(x_prompt)
    y_sample = trunk(x_sample)
    return (y_prompt, y_sample)
```

```python
import functools
import math

import numpy as np
import jax
import jax.numpy as jnp
from jax import lax
from jax.experimental import pallas as pl
from jax.experimental.pallas import tpu as pltpu

F32 = jnp.float32
BF16 = jnp.bfloat16

D_MODEL = 2048
DEPTH = 4
ALPHA = (2 * DEPTH) ** 0.25
LN_EPS = 1e-5
CONV_W = 4
LRU_BLOCKS = 8
LRU_BW = D_MODEL // LRU_BLOCKS
LRU_C = 8.0
SSD_D_INNER = 2 * D_MODEL
SSD_HEAD_DIM = 64
SSD_HEADS = SSD_D_INNER // SSD_HEAD_DIM
SSD_GROUPS = 8
SSD_STATE = 128
SSD_CHUNK = 128
SSD_GROUP_W = SSD_D_INNER // SSD_GROUPS
SSD_CONV_DIM = SSD_D_INNER + 2 * SSD_GROUPS * SSD_STATE
GRID_W = 64
WIN_H = 8
WIN_W = 16
NA_HEAD_DIM = 128
NA_HEADS = D_MODEL // NA_HEAD_DIM
MLP_HIDDEN = 4 * D_MODEL

V7X_SUBLANES = 8
V7X_LANES = 128
V7X_VMEM_LIMIT_BYTES = 56 * 1024 * 1024


def _params(semantics, vmem_bytes=V7X_VMEM_LIMIT_BYTES):
    return pltpu.CompilerParams(dimension_semantics=semantics, vmem_limit_bytes=vmem_bytes)


def _sigmoid(x):
    return 1.0 / (1.0 + jnp.exp(-x))


def _softplus(x):
    return jnp.maximum(x, 0.0) + jnp.log1p(jnp.exp(-jnp.abs(x)))


def _gelu_tanh(x):
    c = math.sqrt(2.0 / math.pi)
    return 0.5 * x * (1.0 + jnp.tanh(c * (x + 0.044715 * (x * x * x))))


def _relu2(x):
    r = jnp.maximum(x, 0.0)
    return r * r


_ACTS = {"none": lambda v: v, "gelu": _gelu_tanh, "relu2": _relu2}


def _conv4(ext, rows, cw_ref, cb_ref):
    n = ext.shape[0]
    lo, hi = V7X_SUBLANES, V7X_SUBLANES + rows
    xm1 = pltpu.roll(ext, 1, 0)[lo:hi]
    xp1 = pltpu.roll(ext, n - 1, 0)[lo:hi]
    xp2 = pltpu.roll(ext, n - 2, 0)[lo:hi]
    x0 = ext[lo:hi]
    return (cb_ref[...] + xm1 * cw_ref[0:1, :] + x0 * cw_ref[1:2, :]
            + xp1 * cw_ref[2:3, :] + xp2 * cw_ref[3:4, :])


def _mm_body(*refs, act, has_bias):
    if has_bias:
        x_ref, w_ref, b_ref, o_ref = refs
    else:
        x_ref, w_ref, o_ref = refs
    acc = jnp.dot(x_ref[...].astype(BF16), w_ref[...], preferred_element_type=F32)
    if has_bias:
        acc = acc + b_ref[...]
    o_ref[...] = _ACTS[act](acc).astype(o_ref.dtype)


def _matmul(x, w, bias=None, *, act="none", out_dtype=F32, bm=1024, bn=1024, name="mm"):
    m, k = x.shape
    k2, n = w.shape
    assert k == k2 and m % bm == 0 and n % bn == 0, (x.shape, w.shape, bm, bn)
    in_specs = [pl.BlockSpec((bm, k), lambda i, j: (i, 0)),
                pl.BlockSpec((k, bn), lambda i, j: (0, j))]
    args = [x, w]
    if bias is not None:
        in_specs.append(pl.BlockSpec((1, bn), lambda i, j: (0, j)))
        args.append(bias)
    return pl.pallas_call(
        functools.partial(_mm_body, act=act, has_bias=bias is not None),
        grid=(m // bm, n // bn),
        in_specs=in_specs,
        out_specs=pl.BlockSpec((bm, bn), lambda i, j: (i, j)),
        out_shape=jax.ShapeDtypeStruct((m, n), out_dtype),
        compiler_params=_params(("parallel", "parallel")),
        name=name,
    )(*args)


def _mm_ln_body(x_ref, w_ref, res_ref, g_ref, b_ref, of_ref, ob_ref, *, nk):
    part = jnp.dot(x_ref[...], w_ref[...], preferred_element_type=F32)

    def finish(acc):
        z = ALPHA * res_ref[...] + acc
        mu = jnp.mean(z, axis=-1, keepdims=True)
        zc = z - mu
        var = jnp.mean(zc * zc, axis=-1, keepdims=True)
        y = zc * lax.rsqrt(var + LN_EPS) * g_ref[...] + b_ref[...]
        of_ref[...] = y
        ob_ref[...] = y.astype(BF16)

    if nk == 1:
        finish(part)
        return
    kk = pl.program_id(1)

    @pl.when(kk == 0)
    def _():
        of_ref[...] = part

    @pl.when(jnp.logical_and(kk > 0, kk < nk - 1))
    def _():
        of_ref[...] += part

    @pl.when(kk == nk - 1)
    def _():
        finish(of_ref[...] + part)


def _matmul_ln(x, w, res, g, b, *, bm=512, bk=2048, name="mm_ln"):
    m, k = x.shape
    k2, n = w.shape
    assert k == k2 and n == D_MODEL and m % bm == 0 and k % bk == 0
    nk = k // bk
    return pl.pallas_call(
        functools.partial(_mm_ln_body, nk=nk),
        grid=(m // bm, nk),
        in_specs=[pl.BlockSpec((bm, bk), lambda i, kk: (i, kk)),
                  pl.BlockSpec((bk, n), lambda i, kk: (kk, 0)),
                  pl.BlockSpec((bm, n), lambda i, kk: (i, 0)),
                  pl.BlockSpec((1, n), lambda i, kk: (0, 0)),
                  pl.BlockSpec((1, n), lambda i, kk: (0, 0))],
        out_specs=[pl.BlockSpec((bm, n), lambda i, kk: (i, 0)),
                   pl.BlockSpec((bm, n), lambda i, kk: (i, 0))],
        out_shape=[jax.ShapeDtypeStruct((m, n), F32), jax.ShapeDtypeStruct((m, n), BF16)],
        compiler_params=_params(("parallel", "arbitrary")),
        name=name,
    )(x, w, res, g, b)


RG_TILE = 256


def _rg_body(*refs, reverse, n_tiles):
    if reverse:
        (u_ref, up_ref, un_ref, cw_ref, cb_ref, wg_ref, ba_ref, bx_ref, lam_ref, hf_ref, gate_ref,
         o_ref, a_s, b_s, carry) = refs
    else:
        (u_ref, up_ref, un_ref, cw_ref, cb_ref, wg_ref, ba_ref, bx_ref, lam_ref,
         o_ref, a_s, b_s, carry) = refs
    t = pl.program_id(1)
    tile = (n_tiles - 1 - t) if reverse else t
    tl = RG_TILE

    @pl.when(t == 0)
    def _():
        carry[...] = jnp.zeros_like(carry)

    prev = up_ref[0] * (tile > 0).astype(F32)
    nxt = un_ref[0] * (tile < n_tiles - 1).astype(F32)
    ext = jnp.concatenate([prev, u_ref[0], nxt], axis=0)
    uc = _conv4(ext, tl, cw_ref, cb_ref)

    nsp = -LRU_C * _softplus(-lam_ref[...])
    for n in range(LRU_BLOCKS):
        sl = slice(n * LRU_BW, (n + 1) * LRU_BW)
        un = uc[:, sl]
        gates = jnp.dot(un.astype(BF16), wg_ref[n], preferred_element_type=F32)
        r = _sigmoid(gates[:, :LRU_BW] + ba_ref[:, sl])
        i = _sigmoid(gates[:, LRU_BW:] + bx_ref[:, sl])
        a = jnp.exp(r * nsp[:, sl])
        a_s[:, sl] = a
        b_s[:, sl] = jnp.sqrt(1.0 - a * a) * (i * un)

    sub = lax.broadcasted_iota(jnp.int32, (V7X_SUBLANES, D_MODEL), 0)
    n_groups = tl // V7X_SUBLANES

    def group(gi, c):
        g = (n_groups - 1 - gi) if reverse else gi
        rows = pl.ds(pl.multiple_of(g * V7X_SUBLANES, V7X_SUBLANES), V7X_SUBLANES)
        a = a_s[rows, :]
        b = b_s[rows, :]
        for s in (1, 2, 4):
            shift = (V7X_SUBLANES - s) if reverse else s
            m = (sub < V7X_SUBLANES - s) if reverse else (sub >= s)
            a_sh = pltpu.roll(a, shift, 0)
            b_sh = pltpu.roll(b, shift, 0)
            b = jnp.where(m, a * b_sh + b, b)
            a = jnp.where(m, a * a_sh, a)
        h = a * c + b
        a_s[rows, :] = h
        edge = h[0:1, :] if reverse else h[V7X_SUBLANES - 1:V7X_SUBLANES, :]
        return jnp.broadcast_to(edge, (V7X_SUBLANES, D_MODEL))

    carry[...] = lax.fori_loop(0, n_groups, group, carry[...])

    if reverse:
        o_ref[0] = ((hf_ref[0] + a_s[...]) * gate_ref[0]).astype(o_ref.dtype)
    else:
        o_ref[0] = a_s[...]


def _rg_scan(u, cw, cb, wg, ba, bx, lam, *, reverse, hf=None, gate=None):
    bsz, seq, d = u.shape
    tl = RG_TILE
    assert seq % tl == 0 and d == D_MODEL
    n_tiles = seq // tl
    hb = tl // V7X_SUBLANES
    n_hblk = seq // V7X_SUBLANES

    def tile_of(t):
        return (n_tiles - 1 - t) if reverse else t

    main = pl.BlockSpec((1, tl, d), lambda b, t: (b, tile_of(t), 0))
    row = lambda r: pl.BlockSpec((r, d), lambda b, t: (0, 0))
    in_specs = [
        main,
        pl.BlockSpec((1, V7X_SUBLANES, d), lambda b, t: (b, jnp.maximum(tile_of(t) * hb - 1, 0), 0)),
        pl.BlockSpec((1, V7X_SUBLANES, d), lambda b, t: (b, jnp.minimum((tile_of(t) + 1) * hb, n_hblk - 1), 0)),
        row(CONV_W), row(1),
        pl.BlockSpec((LRU_BLOCKS, LRU_BW, 2 * LRU_BW), lambda b, t: (0, 0, 0)),
        row(1), row(1), row(1),
    ]
    args = [u, u, u, cw, cb, wg, ba, bx, lam]
    if reverse:
        in_specs += [main, main]
        args += [hf, gate]
    return pl.pallas_call(
        functools.partial(_rg_body, reverse=reverse, n_tiles=n_tiles),
        grid=(bsz, n_tiles),
        in_specs=in_specs,
        out_specs=main,
        out_shape=jax.ShapeDtypeStruct((bsz, seq, d), BF16 if reverse else F32),
        scratch_shapes=[pltpu.VMEM((tl, d), F32), pltpu.VMEM((tl, d), F32), pltpu.VMEM((V7X_SUBLANES, d), F32)],
        compiler_params=_params(("parallel", "arbitrary")),
        name="rg_bwd" if reverse else "rg_fwd",
    )(*args)


def _prefix_rows(x, rid):
    q = x.shape[0]
    s = 1
    while s < q:
        x = x + jnp.where(rid >= s, pltpu.roll(x, s, 0), 0.0)
        s *= 2
    return x


def _suffix_rows(x, rid):
    q = x.shape[0]
    s = 1
    while s < q:
        x = x + jnp.where(rid < q - s, pltpu.roll(x, q - s, 0), 0.0)
        s *= 2
    return x


def _expand_heads(v, r_ref):
    hi = v.astype(BF16)
    lo = (v - hi.astype(F32)).astype(BF16)
    r = r_ref[...]
    return jnp.dot(hi, r, preferred_element_type=F32) + jnp.dot(lo, r, preferred_element_type=F32)


def _ssd_dt(dt_ref, dtb_ref, alog_ref):
    dt = _softplus(dt_ref[0] + dtb_ref[...])
    return dt, dt * (-jnp.exp(alog_ref[...]))


def _ssd1_body(x_ref, xp_ref, xn_ref, dt_ref, cw_ref, cb_ref, dtb_ref, alog_ref, dsk_ref, rf_ref, rb_ref,
               y1_ref, c_ref, sb_ref, hf_s, *, n_chunks):
    q = SSD_CHUNK
    c = pl.program_id(1)

    @pl.when(c == 0)
    def _():
        hf_s[...] = jnp.zeros_like(hf_s)

    prev = xp_ref[0] * (c > 0).astype(F32)
    nxt = xn_ref[0] * (c < n_chunks - 1).astype(F32)
    ext = jnp.concatenate([prev, x_ref[0], nxt], axis=0)
    v = _conv4(ext, q, cw_ref, cb_ref)
    act = v * _sigmoid(v)
    xs = act[:, :SSD_D_INNER]
    bm = act[:, SSD_D_INNER:SSD_D_INNER + SSD_GROUPS * SSD_STATE]
    cm = act[:, SSD_D_INNER + SSD_GROUPS * SSD_STATE:]
    c_ref[0] = cm.astype(BF16)

    dt, dta = _ssd_dt(dt_ref, dtb_ref, alog_ref)
    rid = lax.broadcasted_iota(jnp.int32, (q, q), 0)
    cid = lax.broadcasted_iota(jnp.int32, (q, q), 1)
    is_fwd = cid < SSD_HEADS
    cum = jnp.where(is_fwd, _prefix_rows(dta, rid), _suffix_rows(dta, rid))
    tot = jnp.where(is_fwd[0:1, :], cum[q - 1:q, :], cum[0:1, :])
    e_in = jnp.exp(cum)
    w_end = dt * jnp.exp(tot - cum)

    ef_x = _expand_heads(e_in, rf_ref)
    xf = (xs * _expand_heads(w_end, rf_ref)).astype(BF16)
    xb = (xs * _expand_heads(w_end, rb_ref)).astype(BF16)

    cum_t = cum.T
    dt_t = dt.T
    lower = rid >= cid
    upper = rid <= cid
    neg_inf = jnp.float32(-jnp.inf)
    lane_lo = cid < SSD_HEAD_DIM

    for g in range(SSD_GROUPS):
        ns = slice(g * SSD_STATE, (g + 1) * SSD_STATE)
        gs = slice(g * SSD_GROUP_W, (g + 1) * SSD_GROUP_W)
        bg = bm[:, ns]
        cg16 = cm[:, ns].astype(BF16)
        cb = lax.dot_general(cg16, bg.astype(BF16), (((1,), (1,)), ((), ())), preferred_element_type=F32)
        bgt16 = bg.T.astype(BF16)
        sf_t = jnp.dot(bgt16, xf[:, gs], preferred_element_type=F32)
        sb_ref[0, 0, :, gs] = jnp.dot(bgt16, xb[:, gs], preferred_element_type=F32)
        h_in = hf_s[:, gs]
        y_off = jnp.dot(cg16, h_in.astype(BF16), preferred_element_type=F32) * ef_x[:, gs]
        hf_s[:, gs] = h_in * ef_x[q - 1:q, gs] + sf_t

        for pr in range(SSD_GROUP_W // (2 * SSD_HEAD_DIM)):
            h0 = g * (SSD_HEADS // SSD_GROUPS) + 2 * pr
            ms = []
            for h in (h0, h0 + 1):
                hb = SSD_HEADS + h
                d_f = cum[:, h:h + 1] - cum_t[h:h + 1, :]
                d_b = cum[:, hb:hb + 1] - cum_t[hb:hb + 1, :]
                m_f = jnp.exp(jnp.where(lower, d_f, neg_inf)) * dt_t[h:h + 1, :]
                m_b = jnp.exp(jnp.where(upper, d_b, neg_inf)) * dt_t[hb:hb + 1, :]
                ms.append((cb * (m_f + m_b)).astype(BF16))
            lhs = jnp.concatenate(ms, axis=1)
            ps = slice(h0 * SSD_HEAD_DIM, (h0 + 2) * SSD_HEAD_DIM)
            xp = xs[:, ps]
            rhs = jnp.concatenate([jnp.where(lane_lo, xp, 0.0).astype(BF16),
                                   jnp.where(lane_lo, 0.0, xp).astype(BF16)], axis=0)
            y_diag = jnp.dot(lhs, rhs, preferred_element_type=F32)
            os_ = slice(pr * 2 * SSD_HEAD_DIM, (pr + 1) * 2 * SSD_HEAD_DIM)
            y1_ref[0, :, ps] = y_diag + y_off[:, os_] + xs[:, ps] * dsk_ref[:, ps]


def _ssd2_body(y1_ref, c_ref, sb_ref, dt_ref, z_ref, dtb_ref, alog_ref, rb_ref, ng_ref, o_ref, hb_s):
    q = SSD_CHUNK
    c = pl.program_id(1)

    @pl.when(c == 0)
    def _():
        hb_s[...] = jnp.zeros_like(hb_s)

    _, dta = _ssd_dt(dt_ref, dtb_ref, alog_ref)
    rid = lax.broadcasted_iota(jnp.int32, (q, q), 0)
    eb_x = _expand_heads(jnp.exp(_suffix_rows(dta, rid)), rb_ref)

    for g in range(SSD_GROUPS):
        ns = slice(g * SSD_STATE, (g + 1) * SSD_STATE)
        gs = slice(g * SSD_GROUP_W, (g + 1) * SSD_GROUP_W)
        h_in = hb_s[:, gs]
        y_off = jnp.dot(c_ref[0, :, ns], h_in.astype(BF16), preferred_element_type=F32) * eb_x[:, gs]
        hb_s[:, gs] = h_in * eb_x[0:1, gs] + sb_ref[0, 0, :, gs]
        zz = z_ref[0, :, gs]
        y = (y1_ref[0, :, gs] + y_off) * (zz * _sigmoid(zz))
        ms = jnp.mean(y * y, axis=-1, keepdims=True)
        o_ref[0, :, gs] = (y * lax.rsqrt(ms + LN_EPS) * ng_ref[:, gs]).astype(BF16)


def _ssd_core(xbc, dt_raw, z, cw, cb, dtb, alog, dskip_x, rf, rb, ng):
    bsz, seq, _ = xbc.shape
    q = SSD_CHUNK
    assert seq % q == 0
    nc = seq // q
    hb = q // V7X_SUBLANES
    n_hblk = seq // V7X_SUBLANES
    row = lambda r, w: pl.BlockSpec((r, w), lambda b, c: (0, 0))
    y1, c16, sb = pl.pallas_call(
        functools.partial(_ssd1_body, n_chunks=nc),
        grid=(bsz, nc),
        in_specs=[
            pl.BlockSpec((1, q, SSD_CONV_DIM), lambda b, c: (b, c, 0)),
            pl.BlockSpec((1, V7X_SUBLANES, SSD_CONV_DIM), lambda b, c: (b, jnp.maximum(c * hb - 1, 0), 0)),
            pl.BlockSpec((1, V7X_SUBLANES, SSD_CONV_DIM), lambda b, c: (b, jnp.minimum((c + 1) * hb, n_hblk - 1), 0)),
            pl.BlockSpec((1, q, 2 * SSD_HEADS), lambda b, c: (b, c, 0)),
            row(CONV_W, SSD_CONV_DIM), row(1, SSD_CONV_DIM), row(1, 2 * SSD_HEADS), row(1, 2 * SSD_HEADS),
            row(1, SSD_D_INNER), row(2 * SSD_HEADS, SSD_D_INNER), row(2 * SSD_HEADS, SSD_D_INNER),
        ],
        out_specs=[
            pl.BlockSpec((1, q, SSD_D_INNER), lambda b, c: (b, c, 0)),
            pl.BlockSpec((1, q, SSD_GROUPS * SSD_STATE), lambda b, c: (b, c, 0)),
            pl.BlockSpec((1, 1, SSD_STATE, SSD_D_INNER), lambda b, c: (b, c, 0, 0)),
        ],
        out_shape=[
            jax.ShapeDtypeStruct((bsz, seq, SSD_D_INNER), F32),
            jax.ShapeDtypeStruct((bsz, seq, SSD_GROUPS * SSD_STATE), BF16),
            jax.ShapeDtypeStruct((bsz, nc, SSD_STATE, SSD_D_INNER), F32),
        ],
        scratch_shapes=[pltpu.VMEM((SSD_STATE, SSD_D_INNER), F32)],
        compiler_params=_params(("parallel", "arbitrary")),
        name="ssd_fwd",
    )(xbc, xbc, xbc, dt_raw, cw, cb, dtb, alog, dskip_x, rf, rb)

    rev = lambda b, c: (b, nc - 1 - c, 0)
    return pl.pallas_call(
        _ssd2_body,
        grid=(bsz, nc),
        in_specs=[
            pl.BlockSpec((1, q, SSD_D_INNER), rev),
            pl.BlockSpec((1, q, SSD_GROUPS * SSD_STATE), rev),
            pl.BlockSpec((1, 1, SSD_STATE, SSD_D_INNER), lambda b, c: (b, nc - 1 - c, 0, 0)),
            pl.BlockSpec((1, q, 2 * SSD_HEADS), rev),
            pl.BlockSpec((1, q, SSD_D_INNER), rev),
            row(1, 2 * SSD_HEADS), row(1, 2 * SSD_HEADS), row(2 * SSD_HEADS, SSD_D_INNER), row(1, SSD_D_INNER),
        ],
        out_specs=pl.BlockSpec((1, q, SSD_D_INNER), rev),
        out_shape=jax.ShapeDtypeStruct((bsz, seq, SSD_D_INNER), BF16),
        scratch_shapes=[pltpu.VMEM((SSD_STATE, SSD_D_INNER), F32)],
        compiler_params=_params(("parallel", "arbitrary")),
        name="ssd_bwd",
    )(y1, c16, sb, dt_raw, z, dtb, alog, rb, ng)


NA_KEYS = WIN_H * GRID_W


def _na_body(q_ref, k_ref, v_ref, bias_ref, o_ref, *, rows):
    scale = NA_HEAD_DIM ** -0.5

    def row(r, carry):
        rs = jnp.clip(r - WIN_H // 2, 0, rows - WIN_H)
        dy0 = rs - r + (WIN_H - 1)
        qs = pl.ds(pl.multiple_of(r * GRID_W, GRID_W), GRID_W)
        ks = pl.ds(pl.multiple_of(rs * GRID_W, GRID_W), NA_KEYS)
        s = lax.dot_general(q_ref[0, qs, :], k_ref[0, ks, :], (((1,), (1,)), ((), ())),
                            preferred_element_type=F32)
        s = s * scale + bias_ref[0, dy0]
        p = jnp.exp(s - jnp.max(s, axis=-1, keepdims=True))
        denom = jnp.sum(p, axis=-1, keepdims=True)
        o = jnp.dot(p.astype(BF16), v_ref[0, ks, :], preferred_element_type=F32)
        o_ref[0, qs, :] = (o / denom).astype(o_ref.dtype)
        return carry

    lax.fori_loop(0, rows, row, 0)


def _na_bias_table(rpb):
    qc = np.arange(GRID_W)[:, None]
    kc = np.arange(GRID_W)[None, :]
    wstart = np.clip(qc - WIN_W // 2, 0, GRID_W - WIN_W)
    ok = (kc >= wstart) & (kc < wstart + WIN_W)
    dx = np.clip(kc - qc + WIN_W - 1, 0, 2 * WIN_W - 2)
    per_dy = jnp.where(ok[None, None], rpb[:, :, dx].astype(F32), -jnp.inf)
    dy = np.arange(WIN_H)[:, None] + np.arange(WIN_H)[None, :]
    tab = per_dy[:, dy]
    return jnp.transpose(tab, (0, 1, 3, 2, 4)).reshape(NA_HEADS, WIN_H, GRID_W, NA_KEYS)


def _natten(qkv, bias_tab):
    bsz, seq, _ = qkv.shape
    rows = seq // GRID_W
    assert seq % GRID_W == 0 and rows >= WIN_H
    blk = lambda off: pl.BlockSpec((1, seq, NA_HEAD_DIM), lambda b, h: (b, 0, off + h))
    return pl.pallas_call(
        functools.partial(_na_body, rows=rows),
        grid=(bsz, NA_HEADS),
        in_specs=[blk(0), blk(NA_HEADS), blk(2 * NA_HEADS),
                  pl.BlockSpec((1, WIN_H, GRID_W, NA_KEYS), lambda b, h: (h, 0, 0, 0))],
        out_specs=blk(0),
        out_shape=jax.ShapeDtypeStruct((bsz, seq, D_MODEL), BF16),
        compiler_params=_params(("parallel", "parallel")),
        name="natten",
    )(qkv, qkv, qkv, bias_tab)


def _row(v):
    return v.reshape(1, -1).astype(F32)


def _rg_layer(xf, xb, shape, p):
    bsz, seq = shape
    gate = _matmul(xb, p["w_gate"], act="gelu", name="rg_in_gate").reshape(bsz, seq, D_MODEL)
    u = _matmul(xb, p["w_u"], name="rg_in_u").reshape(bsz, seq, D_MODEL)
    hf = _rg_scan(u, p["cw"], p["cb"], p["wg"][0], p["ba"][0], p["bx"][0], p["lam"][0], reverse=False)
    y = _rg_scan(u, p["cw"], p["cb"], p["wg"][1], p["ba"][1], p["bx"][1], p["lam"][1], reverse=True,
                 hf=hf, gate=gate)
    return y.reshape(bsz * seq, D_MODEL), p["w_out"]


def _ssd_layer(xf, xb, shape, p):
    bsz, seq = shape
    z = _matmul(xb, p["w_z"], name="ssd_in_z").reshape(bsz, seq, SSD_D_INNER)
    xbc = _matmul(xb, p["w_xbc"], name="ssd_in_xbc").reshape(bsz, seq, SSD_CONV_DIM)
    dt = _matmul(xb, p["w_dt"], bn=2 * SSD_HEADS, name="ssd_in_dt").reshape(bsz, seq, 2 * SSD_HEADS)
    y = _ssd_core(xbc, dt, z, p["cw"], p["cb"], p["dtb"], p["alog"], p["dskip_x"], p["rf"], p["rb"], p["ng"])
    return y.reshape(bsz * seq, SSD_D_INNER), p["w_out"]


def _na_layer(xf, xb, shape, p):
    bsz, seq = shape
    qkv = _matmul(xb, p["w_qkv"], p["b_qkv"], out_dtype=BF16, name="na_qkv").reshape(bsz, seq, 3 * D_MODEL)
    o = _natten(qkv, p["bias_tab"])
    return o.reshape(bsz * seq, D_MODEL), p["w_out"]


def _head_expand_matrix(offset):
    r = np.zeros((2 * SSD_HEADS, SSD_D_INNER), np.float32)
    for h in range(SSD_HEADS):
        r[offset + h, h * SSD_HEAD_DIM:(h + 1) * SSD_HEAD_DIM] = 1.0
    return jnp.asarray(r, BF16)


def _rg_params(w_in, conv_w, conv_b, w_a, b_a, w_x, b_x, lam, w_out):
    return dict(w_gate=w_in[:, :D_MODEL].astype(BF16), w_u=w_in[:, D_MODEL:].astype(BF16),
                cw=conv_w, cb=_row(conv_b),
                wg=jnp.concatenate([w_a, w_x], axis=-1).astype(BF16),
                ba=b_a.reshape(2, 1, D_MODEL), bx=b_x.reshape(2, 1, D_MODEL), lam=lam.reshape(2, 1, D_MODEL),
                w_out=w_out.astype(BF16))


def _ssd_params(w_in, conv_w, conv_b, dt_bias, a_log, d_skip, norm_g, w_out):
    return dict(w_z=w_in[:, :SSD_D_INNER].astype(BF16),
                w_xbc=w_in[:, SSD_D_INNER:SSD_D_INNER + SSD_CONV_DIM].astype(BF16),
                w_dt=w_in[:, SSD_D_INNER + SSD_CONV_DIM:].astype(BF16),
                cw=conv_w, cb=_row(conv_b), dtb=_row(dt_bias), alog=_row(a_log),
                dskip_x=_row(jnp.repeat(d_skip, SSD_HEAD_DIM)),
                rf=_head_expand_matrix(0), rb=_head_expand_matrix(SSD_HEADS),
                ng=_row(norm_g), w_out=w_out.astype(BF16))


def _na_params(w_qkv, b_qkv, rpb, w_out):
    return dict(w_qkv=w_qkv.astype(BF16), b_qkv=_row(b_qkv), bias_tab=_na_bias_table(rpb),
                w_out=w_out.astype(BF16))


def kernel(x_prompt, x_sample, rg_w_in, rg_conv_w, rg_conv_b, rg_w_a, rg_b_a, rg_w_x, rg_b_x, rg_lambda, rg_w_out,
           ssd_w_in, ssd_conv_w, ssd_conv_b, ssd_dt_bias, ssd_a_log, ssd_d, ssd_norm_g, ssd_w_out,
           na_w_qkv, na_b_qkv, na_rpb, na_w_out, mlp_w_up, mlp_w_down, ln1_g, ln1_b, ln2_g, ln2_b):
    layers = []
    for i in range(DEPTH):
        kind, j = i % 3, i // 3
        if kind == 0:
            fn, p = _rg_layer, _rg_params(rg_w_in[j], rg_conv_w[j], rg_conv_b[j], rg_w_a[j], rg_b_a[j], rg_w_x[j],
                                          rg_b_x[j], rg_lambda[j], rg_w_out[j])
        elif kind == 1:
            fn, p = _ssd_layer, _ssd_params(ssd_w_in[j], ssd_conv_w[j], ssd_conv_b[j], ssd_dt_bias[j], ssd_a_log[j],
                                            ssd_d[j], ssd_norm_g[j], ssd_w_out[j])
        else:
            fn, p = _na_layer, _na_params(na_w_qkv[j], na_b_qkv[j], na_rpb[j], na_w_out[j])
        layers.append((fn, p, mlp_w_up[i].astype(BF16), mlp_w_down[i].astype(BF16),
                       _row(ln1_g[i]), _row(ln1_b[i]), _row(ln2_g[i]), _row(ln2_b[i])))

    def trunk(x):
        bsz, seq, _ = x.shape
        xf = x.reshape(bsz * seq, D_MODEL)
        xb = xf
        for fn, p, w_up, w_down, g1, b1, g2, b2 in layers:
            y, w_out = fn(xf, xb, (bsz, seq), p)
            xf, xb = _matmul_ln(y, w_out, xf, g1, b1, name="mixer_out_ln")
            hid = _matmul(xb, w_up, act="relu2", out_dtype=BF16, name="mlp_up")
            xf, xb = _matmul_ln(hid, w_down, xf, g2, b2, name="mlp_down_ln")
        return xf.reshape(bsz, seq, D_MODEL)

    return trunk(x_prompt), trunk(x_sample)
```

```python
import functools
import math

import numpy as np
import jax
import jax.numpy as jnp
from jax import lax
from jax.experimental import pallas as pl
from jax.experimental.pallas import tpu as pltpu

F32 = jnp.float32
BF16 = jnp.bfloat16

D_MODEL = 2048
DEPTH = 4
ALPHA = (2 * DEPTH) ** 0.25
LN_EPS = 1e-5
CONV_W = 4
LRU_BLOCKS = 8
LRU_BW = D_MODEL // LRU_BLOCKS
LRU_C = 8.0
SSD_D_INNER = 2 * D_MODEL
SSD_HEAD_DIM = 64
SSD_HEADS = SSD_D_INNER // SSD_HEAD_DIM
SSD_GROUPS = 8
SSD_STATE = 128
SSD_CHUNK = 128
SSD_GROUP_W = SSD_D_INNER // SSD_GROUPS
SSD_CONV_DIM = SSD_D_INNER + 2 * SSD_GROUPS * SSD_STATE
GRID_W = 64
WIN_H = 8
WIN_W = 16
NA_HEAD_DIM = 128
NA_HEADS = D_MODEL // NA_HEAD_DIM
MLP_HIDDEN = 4 * D_MODEL

V7X_SUBLANES = 8
V7X_LANES = 128
V7X_VMEM_LIMIT_BYTES = 56 * 1024 * 1024


def _params(semantics, vmem_bytes=V7X_VMEM_LIMIT_BYTES):
    return pltpu.CompilerParams(dimension_semantics=semantics, vmem_limit_bytes=vmem_bytes)


def _sigmoid(x):
    return 1.0 / (1.0 + jnp.exp(-x))


def _softplus(x):
    return jnp.maximum(x, 0.0) + jnp.log1p(jnp.exp(-jnp.abs(x)))


def _gelu_tanh(x):
    c = math.sqrt(2.0 / math.pi)
    return 0.5 * x * (1.0 + jnp.tanh(c * (x + 0.044715 * (x * x * x))))


def _relu2(x):
    r = jnp.maximum(x, 0.0)
    return r * r


_ACTS = {"none": lambda v: v, "gelu": _gelu_tanh, "relu2": _relu2}


def _conv4(x_ref, prev_ref, next_ref, has_prev, has_next, cw_ref, cb_ref):
    rows = x_ref.shape[1]
    ext = jnp.concatenate([prev_ref[0] * has_prev.astype(F32), x_ref[0], next_ref[0] * has_next.astype(F32)], axis=0)
    n = rows + 2 * V7X_SUBLANES
    lo, hi = V7X_SUBLANES, V7X_SUBLANES + rows
    xm1 = pltpu.roll(ext, 1, 0)[lo:hi]
    xp1 = pltpu.roll(ext, n - 1, 0)[lo:hi]
    xp2 = pltpu.roll(ext, n - 2, 0)[lo:hi]
    return (cb_ref[...] + xm1 * cw_ref[0:1, :] + ext[lo:hi] * cw_ref[1:2, :]
            + xp1 * cw_ref[2:3, :] + xp2 * cw_ref[3:4, :])


def _mm_body(*refs, act, has_bias):
    if has_bias:
        x_ref, w_ref, b_ref, o_ref = refs
    else:
        x_ref, w_ref, o_ref = refs
    acc = jnp.dot(x_ref[...].astype(BF16), w_ref[...], preferred_element_type=F32)
    if has_bias:
        acc = acc + b_ref[...]
    o_ref[...] = _ACTS[act](acc).astype(o_ref.dtype)


def _matmul(x, w, bias=None, *, act="none", out_dtype=F32, bm=1024, bn=1024, name="mm"):
    m, k = x.shape
    k2, n = w.shape
    assert k == k2 and m % bm == 0 and n % bn == 0, (x.shape, w.shape, bm, bn)
    in_specs = [pl.BlockSpec((bm, k), lambda i, j: (i, 0)),
                pl.BlockSpec((k, bn), lambda i, j: (0, j))]
    args = [x, w]
    if bias is not None:
        in_specs.append(pl.BlockSpec((1, bn), lambda i, j: (0, j)))
        args.append(bias)
    return pl.pallas_call(
        functools.partial(_mm_body, act=act, has_bias=bias is not None),
        grid=(m // bm, n // bn),
        in_specs=in_specs,
        out_specs=pl.BlockSpec((bm, bn), lambda i, j: (i, j)),
        out_shape=jax.ShapeDtypeStruct((m, n), out_dtype),
        compiler_params=_params(("parallel", "parallel")),
        name=name,
    )(*args)


LN_ROW_SPLIT = 2


def _mm_ln_body(x_ref, w_ref, res_ref, g_ref, b_ref, of_ref, ob_ref, *, nk):
    def dot(rows):
        return jnp.dot(x_ref[rows, :], w_ref[...], preferred_element_type=F32)

    def finish():
        hm = x_ref.shape[0] // LN_ROW_SPLIT
        for h in range(LN_ROW_SPLIT):
            rows = slice(h * hm, (h + 1) * hm)
            z = ALPHA * res_ref[rows, :] + dot(rows)
            if nk > 1:
                z = z + of_ref[rows, :]
            mu = jnp.mean(z, axis=-1, keepdims=True)
            zc = z - mu
            var = jnp.mean(zc * zc, axis=-1, keepdims=True)
            y = zc * lax.rsqrt(var + LN_EPS) * g_ref[...] + b_ref[...]
            of_ref[rows, :] = y
            ob_ref[rows, :] = y.astype(BF16)

    if nk == 1:
        finish()
        return
    kk = pl.program_id(1)
    every = slice(None)

    @pl.when(kk == 0)
    def _():
        of_ref[...] = dot(every)

    if nk > 2:
        @pl.when(jnp.logical_and(kk > 0, kk < nk - 1))
        def _():
            of_ref[...] += dot(every)

    pl.when(kk == nk - 1)(finish)


def _matmul_ln(x, w, res, g, b, *, bm=512, bk=2048, name="mm_ln"):
    m, k = x.shape
    k2, n = w.shape
    assert k == k2 and n == D_MODEL and m % bm == 0 and k % bk == 0
    nk = k // bk
    return pl.pallas_call(
        functools.partial(_mm_ln_body, nk=nk),
        grid=(m // bm, nk),
        in_specs=[pl.BlockSpec((bm, bk), lambda i, kk: (i, kk)),
                  pl.BlockSpec((bk, n), lambda i, kk: (kk, 0)),
                  pl.BlockSpec((bm, n), lambda i, kk: (i, 0)),
                  pl.BlockSpec((1, n), lambda i, kk: (0, 0)),
                  pl.BlockSpec((1, n), lambda i, kk: (0, 0))],
        out_specs=[pl.BlockSpec((bm, n), lambda i, kk: (i, 0)),
                   pl.BlockSpec((bm, n), lambda i, kk: (i, 0))],
        out_shape=[jax.ShapeDtypeStruct((m, n), F32), jax.ShapeDtypeStruct((m, n), BF16)],
        compiler_params=_params(("parallel", "arbitrary")),
        name=name,
    )(x, w, res, g, b)


RG_TILE = 256
LOG2E = math.log2(math.e)
SQRT_FLOOR = 1e-30


def _rg_gates(uc, wg_ref, ba_ref, bx_ref, lam_ref, a_s, b_s):
    nsp2 = (-LRU_C * LOG2E) * _softplus(-lam_ref[...])
    for n in range(LRU_BLOCKS):
        sl = slice(n * LRU_BW, (n + 1) * LRU_BW)
        un = uc[:, sl]
        gates = jnp.dot(un.astype(BF16), wg_ref[n], preferred_element_type=F32)
        r = _sigmoid(gates[:, :LRU_BW] + ba_ref[:, sl])
        i = _sigmoid(gates[:, LRU_BW:] + bx_ref[:, sl])
        a = jnp.exp2(r * nsp2[:, sl])
        v = 1.0 - a * a
        a_s[:, sl] = a
        b_s[:, sl] = (v * lax.rsqrt(jnp.maximum(v, SQRT_FLOOR))) * (i * un)


def _rg_scan_tile(a_s, b_s, carry, *, reverse):
    sub = lax.broadcasted_iota(jnp.int32, (V7X_SUBLANES, D_MODEL), 0)
    n_groups = a_s.shape[0] // V7X_SUBLANES
    first = (sub == V7X_SUBLANES - 1) if reverse else (sub == 0)

    def group(gi, c):
        g = (n_groups - 1 - gi) if reverse else gi
        rows = pl.ds(pl.multiple_of(g * V7X_SUBLANES, V7X_SUBLANES), V7X_SUBLANES)
        a = a_s[rows, :]
        b = b_s[rows, :]
        b = b + jnp.where(first, a * c, 0.0)
        for s in (1, 2, 4):
            shift = (V7X_SUBLANES - s) if reverse else s
            m = (sub < V7X_SUBLANES - s) if reverse else (sub >= s)
            b = jnp.where(m, a * pltpu.roll(b, shift, 0) + b, b)
            if s < 4:
                a = jnp.where(m, a * pltpu.roll(a, shift, 0), a)
        a_s[rows, :] = b
        edge = b[0:1, :] if reverse else b[V7X_SUBLANES - 1:V7X_SUBLANES, :]
        return jnp.broadcast_to(edge, (V7X_SUBLANES, D_MODEL))

    carry[...] = lax.fori_loop(0, n_groups, group, carry[...])


def _rg_fwd_body(u_ref, up_ref, un_ref, cw_ref, cb_ref, wg_ref, ba_ref, bx_ref, lam_ref,
                 hf_ref, uc_ref, a_s, b_s, carry, *, n_tiles):
    t = pl.program_id(1)

    @pl.when(t == 0)
    def _():
        carry[...] = jnp.zeros_like(carry)

    uc = _conv4(u_ref, up_ref, un_ref, t > 0, t < n_tiles - 1, cw_ref, cb_ref)
    uc_ref[0] = uc
    _rg_gates(uc, wg_ref, ba_ref, bx_ref, lam_ref, a_s, b_s)
    _rg_scan_tile(a_s, b_s, carry, reverse=False)
    hf_ref[0] = a_s[...]


def _rg_bwd_body(uc_ref, wg_ref, ba_ref, bx_ref, lam_ref, hf_ref, gate_ref, y_ref, a_s, b_s, carry):
    @pl.when(pl.program_id(1) == 0)
    def _():
        carry[...] = jnp.zeros_like(carry)

    _rg_gates(uc_ref[0], wg_ref, ba_ref, bx_ref, lam_ref, a_s, b_s)
    _rg_scan_tile(a_s, b_s, carry, reverse=True)
    y_ref[0] = ((hf_ref[0] + a_s[...]) * gate_ref[0]).astype(y_ref.dtype)


def _rg_core(u, gate, p):
    bsz, seq, d = u.shape
    tl = RG_TILE
    assert seq % tl == 0 and d == D_MODEL
    n_tiles = seq // tl
    hb = tl // V7X_SUBLANES
    n_hblk = seq // V7X_SUBLANES
    row = lambda r: pl.BlockSpec((r, d), lambda b, t: (0, 0))
    gate_w = pl.BlockSpec((LRU_BLOCKS, LRU_BW, 2 * LRU_BW), lambda b, t: (0, 0, 0))
    scan_scratch = [pltpu.VMEM((tl, d), F32), pltpu.VMEM((tl, d), F32), pltpu.VMEM((V7X_SUBLANES, d), F32)]

    fwd = pl.BlockSpec((1, tl, d), lambda b, t: (b, t, 0))
    hf, uc = pl.pallas_call(
        functools.partial(_rg_fwd_body, n_tiles=n_tiles),
        grid=(bsz, n_tiles),
        in_specs=[fwd,
                  pl.BlockSpec((1, V7X_SUBLANES, d), lambda b, t: (b, jnp.maximum(t * hb - 1, 0), 0)),
                  pl.BlockSpec((1, V7X_SUBLANES, d), lambda b, t: (b, jnp.minimum((t + 1) * hb, n_hblk - 1), 0)),
                  row(CONV_W), row(1), gate_w, row(1), row(1), row(1)],
        out_specs=[fwd, fwd],
        out_shape=[jax.ShapeDtypeStruct((bsz, seq, d), F32), jax.ShapeDtypeStruct((bsz, seq, d), F32)],
        scratch_shapes=scan_scratch,
        compiler_params=_params(("parallel", "arbitrary")),
        name="rg_fwd",
    )(u, u, u, p["cw"], p["cb"], p["wg"][0], p["ba"][0], p["bx"][0], p["lam"][0])

    bwd = pl.BlockSpec((1, tl, d), lambda b, t: (b, n_tiles - 1 - t, 0))
    return pl.pallas_call(
        _rg_bwd_body,
        grid=(bsz, n_tiles),
        in_specs=[bwd, gate_w, row(1), row(1), row(1), bwd, bwd],
        out_specs=bwd,
        out_shape=jax.ShapeDtypeStruct((bsz, seq, d), BF16),
        scratch_shapes=scan_scratch,
        compiler_params=_params(("parallel", "arbitrary")),
        name="rg_bwd",
    )(uc, p["wg"][1], p["ba"][1], p["bx"][1], p["lam"][1], hf, gate)


def _prefix_rows(x, rid):
    q = x.shape[0]
    s = 1
    while s < q:
        x = x + jnp.where(rid >= s, pltpu.roll(x, s, 0), 0.0)
        s *= 2
    return x


def _suffix_rows(x, rid):
    q = x.shape[0]
    s = 1
    while s < q:
        x = x + jnp.where(rid < q - s, pltpu.roll(x, q - s, 0), 0.0)
        s *= 2
    return x


def _expand_heads(v, r_ref):
    hi = v.astype(BF16)
    lo = (v - hi.astype(F32)).astype(BF16)
    r = r_ref[...]
    return jnp.dot(hi, r, preferred_element_type=F32) + jnp.dot(lo, r, preferred_element_type=F32)


def _ssd_dt(dt_ref, dtb_ref, alog_ref):
    dt = _softplus(dt_ref[0] + dtb_ref[...])
    return dt, dt * (-jnp.exp(alog_ref[...]))


def _ssd1_body(x_ref, xp_ref, xn_ref, dt_ref, cw_ref, cb_ref, dtb_ref, alog_ref, dsk_ref, rf_ref, rb_ref,
               y1_ref, c_ref, sb_ref, hf_s, *, n_chunks):
    q = SSD_CHUNK
    c = pl.program_id(1)

    @pl.when(c == 0)
    def _():
        hf_s[...] = jnp.zeros_like(hf_s)

    v = _conv4(x_ref, xp_ref, xn_ref, c > 0, c < n_chunks - 1, cw_ref, cb_ref)
    act = v * _sigmoid(v)
    xs = act[:, :SSD_D_INNER]
    bm = act[:, SSD_D_INNER:SSD_D_INNER + SSD_GROUPS * SSD_STATE]
    cm = act[:, SSD_D_INNER + SSD_GROUPS * SSD_STATE:]
    c_ref[0] = cm.astype(BF16)

    dt, dta = _ssd_dt(dt_ref, dtb_ref, alog_ref)
    rid = lax.broadcasted_iota(jnp.int32, (q, q), 0)
    cid = lax.broadcasted_iota(jnp.int32, (q, q), 1)
    is_fwd = cid < SSD_HEADS
    cum = jnp.where(is_fwd, _prefix_rows(dta, rid), _suffix_rows(dta, rid))
    tot = jnp.where(is_fwd[0:1, :], cum[q - 1:q, :], cum[0:1, :])
    e_in = jnp.exp(cum)
    w_end = dt * jnp.exp(tot - cum)

    ef_x = _expand_heads(e_in, rf_ref)
    xf = (xs * _expand_heads(w_end, rf_ref)).astype(BF16)
    xb = (xs * _expand_heads(w_end, rb_ref)).astype(BF16)

    dst = cum * LOG2E
    src_t = (dst - jnp.log2(dt)).T
    lower = rid >= cid
    upper = rid <= cid
    neg_inf = jnp.float32(-jnp.inf)
    lane_lo = cid < SSD_HEAD_DIM

    for g in range(SSD_GROUPS):
        ns = slice(g * SSD_STATE, (g + 1) * SSD_STATE)
        gs = slice(g * SSD_GROUP_W, (g + 1) * SSD_GROUP_W)
        bg = bm[:, ns]
        cg16 = cm[:, ns].astype(BF16)
        cb = lax.dot_general(cg16, bg.astype(BF16), (((1,), (1,)), ((), ())), preferred_element_type=F32)
        bgt16 = bg.T.astype(BF16)
        sf_t = jnp.dot(bgt16, xf[:, gs], preferred_element_type=F32)
        sb_ref[0, 0, :, gs] = jnp.dot(bgt16, xb[:, gs], preferred_element_type=F32)
        h_in = hf_s[:, gs]
        y_off = jnp.dot(cg16, h_in.astype(BF16), preferred_element_type=F32) * ef_x[:, gs]
        hf_s[:, gs] = h_in * ef_x[q - 1:q, gs] + sf_t

        for pr in range(SSD_GROUP_W // (2 * SSD_HEAD_DIM)):
            h0 = g * (SSD_HEADS // SSD_GROUPS) + 2 * pr
            ms = []
            for h in (h0, h0 + 1):
                hb = SSD_HEADS + h
                d_f = dst[:, h:h + 1] - src_t[h:h + 1, :]
                d_b = dst[:, hb:hb + 1] - src_t[hb:hb + 1, :]
                m_f = jnp.exp2(jnp.where(lower, d_f, neg_inf))
                m_b = jnp.exp2(jnp.where(upper, d_b, neg_inf))
                ms.append((cb * (m_f + m_b)).astype(BF16))
            lhs = jnp.concatenate(ms, axis=1)
            ps = slice(h0 * SSD_HEAD_DIM, (h0 + 2) * SSD_HEAD_DIM)
            xp = xs[:, ps]
            rhs = jnp.concatenate([jnp.where(lane_lo, xp, 0.0).astype(BF16),
                                   jnp.where(lane_lo, 0.0, xp).astype(BF16)], axis=0)
            y_diag = jnp.dot(lhs, rhs, preferred_element_type=F32)
            os_ = slice(pr * 2 * SSD_HEAD_DIM, (pr + 1) * 2 * SSD_HEAD_DIM)
            y1_ref[0, :, ps] = y_diag + y_off[:, os_] + xs[:, ps] * dsk_ref[:, ps]


def _ssd2_body(y1_ref, c_ref, sb_ref, dt_ref, z_ref, dtb_ref, alog_ref, rb_ref, ng_ref, o_ref, hb_s):
    q = SSD_CHUNK
    c = pl.program_id(1)

    @pl.when(c == 0)
    def _():
        hb_s[...] = jnp.zeros_like(hb_s)

    _, dta = _ssd_dt(dt_ref, dtb_ref, alog_ref)
    rid = lax.broadcasted_iota(jnp.int32, (q, q), 0)
    eb_x = _expand_heads(jnp.exp(_suffix_rows(dta, rid)), rb_ref)

    for g in range(SSD_GROUPS):
        ns = slice(g * SSD_STATE, (g + 1) * SSD_STATE)
        gs = slice(g * SSD_GROUP_W, (g + 1) * SSD_GROUP_W)
        h_in = hb_s[:, gs]
        y_off = jnp.dot(c_ref[0, :, ns], h_in.astype(BF16), preferred_element_type=F32) * eb_x[:, gs]
        hb_s[:, gs] = h_in * eb_x[0:1, gs] + sb_ref[0, 0, :, gs]
        zz = z_ref[0, :, gs]
        y = (y1_ref[0, :, gs] + y_off) * (zz * _sigmoid(zz))
        ms = jnp.mean(y * y, axis=-1, keepdims=True)
        o_ref[0, :, gs] = (y * lax.rsqrt(ms + LN_EPS) * ng_ref[:, gs]).astype(BF16)


def _ssd_core(xbc, dt_raw, z, cw, cb, dtb, alog, dskip_x, rf, rb, ng):
    bsz, seq, _ = xbc.shape
    q = SSD_CHUNK
    assert seq % q == 0
    nc = seq // q
    hb = q // V7X_SUBLANES
    n_hblk = seq // V7X_SUBLANES
    row = lambda r, w: pl.BlockSpec((r, w), lambda b, c: (0, 0))
    y1, c16, sb = pl.pallas_call(
        functools.partial(_ssd1_body, n_chunks=nc),
        grid=(bsz, nc),
        in_specs=[
            pl.BlockSpec((1, q, SSD_CONV_DIM), lambda b, c: (b, c, 0)),
            pl.BlockSpec((1, V7X_SUBLANES, SSD_CONV_DIM), lambda b, c: (b, jnp.maximum(c * hb - 1, 0), 0)),
            pl.BlockSpec((1, V7X_SUBLANES, SSD_CONV_DIM), lambda b, c: (b, jnp.minimum((c + 1) * hb, n_hblk - 1), 0)),
            pl.BlockSpec((1, q, 2 * SSD_HEADS), lambda b, c: (b, c, 0)),
            row(CONV_W, SSD_CONV_DIM), row(1, SSD_CONV_DIM), row(1, 2 * SSD_HEADS), row(1, 2 * SSD_HEADS),
            row(1, SSD_D_INNER), row(2 * SSD_HEADS, SSD_D_INNER), row(2 * SSD_HEADS, SSD_D_INNER),
        ],
        out_specs=[
            pl.BlockSpec((1, q, SSD_D_INNER), lambda b, c: (b, c, 0)),
            pl.BlockSpec((1, q, SSD_GROUPS * SSD_STATE), lambda b, c: (b, c, 0)),
            pl.BlockSpec((1, 1, SSD_STATE, SSD_D_INNER), lambda b, c: (b, c, 0, 0)),
        ],
        out_shape=[
            jax.ShapeDtypeStruct((bsz, seq, SSD_D_INNER), F32),
            jax.ShapeDtypeStruct((bsz, seq, SSD_GROUPS * SSD_STATE), BF16),
            jax.ShapeDtypeStruct((bsz, nc, SSD_STATE, SSD_D_INNER), F32),
        ],
        scratch_shapes=[pltpu.VMEM((SSD_STATE, SSD_D_INNER), F32)],
        compiler_params=_params(("parallel", "arbitrary")),
        name="ssd_fwd",
    )(xbc, xbc, xbc, dt_raw, cw, cb, dtb, alog, dskip_x, rf, rb)

    rev = lambda b, c: (b, nc - 1 - c, 0)
    return pl.pallas_call(
        _ssd2_body,
        grid=(bsz, nc),
        in_specs=[
            pl.BlockSpec((1, q, SSD_D_INNER), rev),
            pl.BlockSpec((1, q, SSD_GROUPS * SSD_STATE), rev),
            pl.BlockSpec((1, 1, SSD_STATE, SSD_D_INNER), lambda b, c: (b, nc - 1 - c, 0, 0)),
            pl.BlockSpec((1, q, 2 * SSD_HEADS), rev),
            pl.BlockSpec((1, q, SSD_D_INNER), rev),
            row(1, 2 * SSD_HEADS), row(1, 2 * SSD_HEADS), row(2 * SSD_HEADS, SSD_D_INNER), row(1, SSD_D_INNER),
        ],
        out_specs=pl.BlockSpec((1, q, SSD_D_INNER), rev),
        out_shape=jax.ShapeDtypeStruct((bsz, seq, SSD_D_INNER), BF16),
        scratch_shapes=[pltpu.VMEM((SSD_STATE, SSD_D_INNER), F32)],
        compiler_params=_params(("parallel", "arbitrary")),
        name="ssd_bwd",
    )(y1, c16, sb, dt_raw, z, dtb, alog, rb, ng)


NA_KEYS = WIN_H * GRID_W


NA_ROWS_PER_STEP = 16


def _na_body(q_ref, k_ref, v_ref, bias_ref, o_ref, *, rows):
    scale = NA_HEAD_DIM ** -0.5

    def step(i, carry):
        qss, kss, scores, probs = [], [], [], []
        for j in range(NA_ROWS_PER_STEP):
            r = i * NA_ROWS_PER_STEP + j
            rs = jnp.clip(r - WIN_H // 2, 0, rows - WIN_H)
            dy0 = rs - r + (WIN_H - 1)
            qs = pl.ds(pl.multiple_of(r * GRID_W, GRID_W), GRID_W)
            ks = pl.ds(pl.multiple_of(rs * GRID_W, GRID_W), NA_KEYS)
            s = lax.dot_general(q_ref[0, qs, :], k_ref[0, ks, :], (((1,), (1,)), ((), ())),
                                preferred_element_type=F32)
            scores.append(s * scale + bias_ref[0, dy0])
            qss.append(qs)
            kss.append(ks)
        for s in scores:
            p = jnp.exp(s - jnp.max(s, axis=-1, keepdims=True))
            probs.append((p.astype(BF16), 1.0 / jnp.sum(p, axis=-1, keepdims=True)))
        for qs, ks, (p, inv) in zip(qss, kss, probs):
            o = jnp.dot(p, v_ref[0, ks, :], preferred_element_type=F32)
            o_ref[0, qs, :] = (o * inv).astype(o_ref.dtype)
        return carry

    lax.fori_loop(0, rows // NA_ROWS_PER_STEP, step, 0)


def _na_bias_table(rpb):
    qc = np.arange(GRID_W)[:, None]
    kc = np.arange(GRID_W)[None, :]
    wstart = np.clip(qc - WIN_W // 2, 0, GRID_W - WIN_W)
    ok = (kc >= wstart) & (kc < wstart + WIN_W)
    dx = np.clip(kc - qc + WIN_W - 1, 0, 2 * WIN_W - 2)
    per_dy = jnp.where(ok[None, None], rpb[:, :, dx].astype(F32), -jnp.inf)
    dy = np.arange(WIN_H)[:, None] + np.arange(WIN_H)[None, :]
    tab = per_dy[:, dy]
    return jnp.transpose(tab, (0, 1, 3, 2, 4)).reshape(NA_HEADS, WIN_H, GRID_W, NA_KEYS)


def _natten(qkv, bias_tab):
    bsz, seq, _ = qkv.shape
    rows = seq // GRID_W
    assert seq % GRID_W == 0 and rows >= WIN_H and rows % NA_ROWS_PER_STEP == 0
    blk = lambda off: pl.BlockSpec((1, seq, NA_HEAD_DIM), lambda b, h: (b, 0, off + h))
    return pl.pallas_call(
        functools.partial(_na_body, rows=rows),
        grid=(bsz, NA_HEADS),
        in_specs=[blk(0), blk(NA_HEADS), blk(2 * NA_HEADS),
                  pl.BlockSpec((1, WIN_H, GRID_W, NA_KEYS), lambda b, h: (h, 0, 0, 0))],
        out_specs=blk(0),
        out_shape=jax.ShapeDtypeStruct((bsz, seq, D_MODEL), BF16),
        compiler_params=_params(("parallel", "parallel")),
        name="natten",
    )(qkv, qkv, qkv, bias_tab)


def _row(v):
    return v.reshape(1, -1).astype(F32)


def _rg_layer(xf, xb, shape, p):
    bsz, seq = shape
    gate = _matmul(xb, p["w_gate"], act="gelu", name="rg_in_gate").reshape(bsz, seq, D_MODEL)
    u = _matmul(xb, p["w_u"], name="rg_in_u").reshape(bsz, seq, D_MODEL)
    y = _rg_core(u, gate, p)
    return y.reshape(bsz * seq, D_MODEL), p["w_out"]


def _ssd_layer(xf, xb, shape, p):
    bsz, seq = shape
    z = _matmul(xb, p["w_z"], name="ssd_in_z").reshape(bsz, seq, SSD_D_INNER)
    xbc = _matmul(xb, p["w_xbc"], name="ssd_in_xbc").reshape(bsz, seq, SSD_CONV_DIM)
    dt = _matmul(xb, p["w_dt"], bn=2 * SSD_HEADS, name="ssd_in_dt").reshape(bsz, seq, 2 * SSD_HEADS)
    y = _ssd_core(xbc, dt, z, p["cw"], p["cb"], p["dtb"], p["alog"], p["dskip_x"], p["rf"], p["rb"], p["ng"])
    return y.reshape(bsz * seq, SSD_D_INNER), p["w_out"]


def _na_layer(xf, xb, shape, p):
    bsz, seq = shape
    qkv = _matmul(xb, p["w_qkv"], p["b_qkv"], out_dtype=BF16, name="na_qkv").reshape(bsz, seq, 3 * D_MODEL)
    o = _natten(qkv, p["bias_tab"])
    return o.reshape(bsz * seq, D_MODEL), p["w_out"]


def _head_expand_matrix(offset):
    r = np.zeros((2 * SSD_HEADS, SSD_D_INNER), np.float32)
    for h in range(SSD_HEADS):
        r[offset + h, h * SSD_HEAD_DIM:(h + 1) * SSD_HEAD_DIM] = 1.0
    return jnp.asarray(r, BF16)


def _rg_params(w_in, conv_w, conv_b, w_a, b_a, w_x, b_x, lam, w_out):
    return dict(w_gate=w_in[:, :D_MODEL].astype(BF16), w_u=w_in[:, D_MODEL:].astype(BF16),
                cw=conv_w, cb=_row(conv_b),
                wg=jnp.concatenate([w_a, w_x], axis=-1).astype(BF16),
                ba=b_a.reshape(2, 1, D_MODEL), bx=b_x.reshape(2, 1, D_MODEL), lam=lam.reshape(2, 1, D_MODEL),
                w_out=w_out.astype(BF16))


def _ssd_params(w_in, conv_w, conv_b, dt_bias, a_log, d_skip, norm_g, w_out):
    return dict(w_z=w_in[:, :SSD_D_INNER].astype(BF16),
                w_xbc=w_in[:, SSD_D_INNER:SSD_D_INNER + SSD_CONV_DIM].astype(BF16),
                w_dt=w_in[:, SSD_D_INNER + SSD_CONV_DIM:].astype(BF16),
                cw=conv_w, cb=_row(conv_b), dtb=_row(dt_bias), alog=_row(a_log),
                dskip_x=_row(jnp.repeat(d_skip, SSD_HEAD_DIM)),
                rf=_head_expand_matrix(0), rb=_head_expand_matrix(SSD_HEADS),
                ng=_row(norm_g), w_out=w_out.astype(BF16))


def _na_params(w_qkv, b_qkv, rpb, w_out):
    return dict(w_qkv=w_qkv.astype(BF16), b_qkv=_row(b_qkv), bias_tab=_na_bias_table(rpb),
                w_out=w_out.astype(BF16))


def kernel(x_prompt, x_sample, rg_w_in, rg_conv_w, rg_conv_b, rg_w_a, rg_b_a, rg_w_x, rg_b_x, rg_lambda, rg_w_out,
           ssd_w_in, ssd_conv_w, ssd_conv_b, ssd_dt_bias, ssd_a_log, ssd_d, ssd_norm_g, ssd_w_out,
           na_w_qkv, na_b_qkv, na_rpb, na_w_out, mlp_w_up, mlp_w_down, ln1_g, ln1_b, ln2_g, ln2_b):
    layers = []
    for i in range(DEPTH):
        kind, j = i % 3, i // 3
        if kind == 0:
            fn, p = _rg_layer, _rg_params(rg_w_in[j], rg_conv_w[j], rg_conv_b[j], rg_w_a[j], rg_b_a[j], rg_w_x[j],
                                          rg_b_x[j], rg_lambda[j], rg_w_out[j])
        elif kind == 1:
            fn, p = _ssd_layer, _ssd_params(ssd_w_in[j], ssd_conv_w[j], ssd_conv_b[j], ssd_dt_bias[j], ssd_a_log[j],
                                            ssd_d[j], ssd_norm_g[j], ssd_w_out[j])
        else:
            fn, p = _na_layer, _na_params(na_w_qkv[j], na_b_qkv[j], na_rpb[j], na_w_out[j])
        layers.append((fn, p, mlp_w_up[i].astype(BF16), mlp_w_down[i].astype(BF16),
                       _row(ln1_g[i]), _row(ln1_b[i]), _row(ln2_g[i]), _row(ln2_b[i])))

    def trunk(x):
        bsz, seq, _ = x.shape
        xf = x.reshape(bsz * seq, D_MODEL)
        xb = xf
        for fn, p, w_up, w_down, g1, b1, g2, b2 in layers:
            y, w_out = fn(xf, xb, (bsz, seq), p)
            xf, xb = _matmul_ln(y, w_out, xf, g1, b1, name="mixer_out_ln")
            hid = _matmul(xb, w_up, act="relu2", out_dtype=BF16, name="mlp_up")
            xf, xb = _matmul_ln(hid, w_down, xf, g2, b2, name="mlp_down_ln")
        return xf.reshape(bsz, seq, D_MODEL)

    return trunk(x_prompt), trunk(x_sample)
```

```python
import functools
import math

import numpy as np
import jax
import jax.numpy as jnp
from jax import lax
from jax.experimental import pallas as pl
from jax.experimental.pallas import tpu as pltpu

F32 = jnp.float32
BF16 = jnp.bfloat16

D_MODEL = 2048
DEPTH = 4
ALPHA = (2 * DEPTH) ** 0.25
LN_EPS = 1e-5
CONV_W = 4
LRU_BLOCKS = 8
LRU_BW = D_MODEL // LRU_BLOCKS
LRU_C = 8.0
SSD_D_INNER = 2 * D_MODEL
SSD_HEAD_DIM = 64
SSD_HEADS = SSD_D_INNER // SSD_HEAD_DIM
SSD_GROUPS = 8
SSD_STATE = 128
SSD_CHUNK = 128
SSD_GROUP_W = SSD_D_INNER // SSD_GROUPS
SSD_CONV_DIM = SSD_D_INNER + 2 * SSD_GROUPS * SSD_STATE
GRID_W = 64
WIN_H = 8
WIN_W = 16
NA_HEAD_DIM = 128
NA_HEADS = D_MODEL // NA_HEAD_DIM
MLP_HIDDEN = 4 * D_MODEL

V7X_SUBLANES = 8
V7X_LANES = 128
V7X_VMEM_LIMIT_BYTES = 56 * 1024 * 1024


def _params(semantics, vmem_bytes=V7X_VMEM_LIMIT_BYTES):
    return pltpu.CompilerParams(dimension_semantics=semantics, vmem_limit_bytes=vmem_bytes)


def _sigmoid(x):
    return 1.0 / (1.0 + jnp.exp(-x))


def _softplus(x):
    return jnp.maximum(x, 0.0) + jnp.log1p(jnp.exp(-jnp.abs(x)))


def _gelu_tanh(x):
    c = math.sqrt(2.0 / math.pi)
    return 0.5 * x * (1.0 + jnp.tanh(c * (x + 0.044715 * (x * x * x))))


def _relu2(x):
    r = jnp.maximum(x, 0.0)
    return r * r


def _silu(x):
    return x * _sigmoid(x)


_ACTS = {"none": lambda v: v, "gelu": _gelu_tanh, "relu2": _relu2, "silu": _silu}


def _mm_body(*refs, act, has_bias):
    if has_bias:
        x_ref, w_ref, b_ref, o_ref = refs
    else:
        x_ref, w_ref, o_ref = refs
    acc = jnp.dot(x_ref[...].astype(BF16), w_ref[...], preferred_element_type=F32)
    if has_bias:
        acc = acc + b_ref[...]
    o_ref[...] = _ACTS[act](acc).astype(o_ref.dtype)


def _matmul(x, w, bias=None, *, act="none", out_dtype=F32, bm=1024, bn=1024, name="mm"):
    m, k = x.shape
    k2, n = w.shape
    assert k == k2 and m % bm == 0 and n % bn == 0, (x.shape, w.shape, bm, bn)
    in_specs = [pl.BlockSpec((bm, k), lambda i, j: (i, 0)),
                pl.BlockSpec((k, bn), lambda i, j: (0, j))]
    args = [x, w]
    if bias is not None:
        in_specs.append(pl.BlockSpec((1, bn), lambda i, j: (0, j)))
        args.append(bias)
    return pl.pallas_call(
        functools.partial(_mm_body, act=act, has_bias=bias is not None),
        grid=(m // bm, n // bn),
        in_specs=in_specs,
        out_specs=pl.BlockSpec((bm, bn), lambda i, j: (i, j)),
        out_shape=jax.ShapeDtypeStruct((m, n), out_dtype),
        compiler_params=_params(("parallel", "parallel")),
        name=name,
    )(*args)


CONV_HALO = 16
CONV_SUBDOT = 256


def _mm_conv_body(x_ref, xp_ref, xn_ref, w_ref, cw_ref, cb_ref, o_ref, *, act, n_tiles):
    t = pl.program_id(1)
    rows = x_ref.shape[1]
    n = rows + 2 * CONV_HALO
    lo, hi = CONV_HALO, CONV_HALO + rows
    ext = jnp.concatenate([xp_ref[0] * (t > 0).astype(xp_ref.dtype), x_ref[0],
                           xn_ref[0] * (t < n_tiles - 1).astype(xn_ref.dtype)], axis=0).astype(BF16)
    for c in range(o_ref.shape[2] // CONV_SUBDOT):
        cols = slice(c * CONV_SUBDOT, (c + 1) * CONV_SUBDOT)
        y = jnp.dot(ext, w_ref[:, cols], preferred_element_type=F32)
        v = (cb_ref[:, cols] + pltpu.roll(y, 1, 0)[lo:hi] * cw_ref[0:1, cols] + y[lo:hi] * cw_ref[1:2, cols]
             + pltpu.roll(y, n - 1, 0)[lo:hi] * cw_ref[2:3, cols] + pltpu.roll(y, n - 2, 0)[lo:hi] * cw_ref[3:4, cols])
        o_ref[0, :, cols] = _ACTS[act](v).astype(o_ref.dtype)


def _matmul_conv(x, w, cw, cb, *, act="none", tl=1024, bn=1024, name="mm_conv"):
    bsz, seq, k = x.shape
    n = w.shape[1]
    assert seq % tl == 0 and n % bn == 0 and bn % CONV_SUBDOT == 0 and tl % CONV_HALO == 0
    n_tiles = seq // tl
    hb = tl // CONV_HALO
    n_hblk = seq // CONV_HALO
    return pl.pallas_call(
        functools.partial(_mm_conv_body, act=act, n_tiles=n_tiles),
        grid=(bsz, n_tiles, n // bn),
        in_specs=[pl.BlockSpec((1, tl, k), lambda b, t, j: (b, t, 0)),
                  pl.BlockSpec((1, CONV_HALO, k), lambda b, t, j: (b, jnp.maximum(t * hb - 1, 0), 0)),
                  pl.BlockSpec((1, CONV_HALO, k), lambda b, t, j: (b, jnp.minimum((t + 1) * hb, n_hblk - 1), 0)),
                  pl.BlockSpec((k, bn), lambda b, t, j: (0, j)),
                  pl.BlockSpec((CONV_W, bn), lambda b, t, j: (0, j)),
                  pl.BlockSpec((1, bn), lambda b, t, j: (0, j))],
        out_specs=pl.BlockSpec((1, tl, bn), lambda b, t, j: (b, t, j)),
        out_shape=jax.ShapeDtypeStruct((bsz, seq, n), F32),
        compiler_params=_params(("parallel", "parallel", "parallel")),
        name=name,
    )(x, x, x, w, cw, cb)


LN_ROW_SPLIT = 4


def _mm_ln_body(x_ref, w_ref, res_ref, g_ref, b_ref, o_ref, *, nk):
    def dot(rows):
        return jnp.dot(x_ref[rows, :], w_ref[...], preferred_element_type=F32)

    def finish():
        hm = x_ref.shape[0] // LN_ROW_SPLIT
        for h in range(LN_ROW_SPLIT):
            rows = slice(h * hm, (h + 1) * hm)
            z = ALPHA * res_ref[rows, :] + dot(rows)
            if nk > 1:
                z = z + o_ref[rows, :]
            mu = jnp.mean(z, axis=-1, keepdims=True)
            zc = z - mu
            var = jnp.mean(zc * zc, axis=-1, keepdims=True)
            o_ref[rows, :] = zc * lax.rsqrt(var + LN_EPS) * g_ref[...] + b_ref[...]

    if nk == 1:
        finish()
        return
    kk = pl.program_id(1)
    every = slice(None)

    @pl.when(kk == 0)
    def _():
        o_ref[...] = dot(every)

    if nk > 2:
        @pl.when(jnp.logical_and(kk > 0, kk < nk - 1))
        def _():
            o_ref[...] += dot(every)

    pl.when(kk == nk - 1)(finish)


def _matmul_ln(x, w, res, g, b, *, bm=1024, bk=1024, name="mm_ln"):
    m, k = x.shape
    k2, n = w.shape
    assert k == k2 and n == D_MODEL and m % bm == 0 and k % bk == 0 and bm % LN_ROW_SPLIT == 0
    nk = k // bk
    return pl.pallas_call(
        functools.partial(_mm_ln_body, nk=nk),
        grid=(m // bm, nk),
        in_specs=[pl.BlockSpec((bm, bk), lambda i, kk: (i, kk)),
                  pl.BlockSpec((bk, n), lambda i, kk: (kk, 0)),
                  pl.BlockSpec((bm, n), lambda i, kk: (i, 0)),
                  pl.BlockSpec((1, n), lambda i, kk: (0, 0)),
                  pl.BlockSpec((1, n), lambda i, kk: (0, 0))],
        out_specs=pl.BlockSpec((bm, n), lambda i, kk: (i, 0)),
        out_shape=jax.ShapeDtypeStruct((m, n), F32),
        compiler_params=_params(("parallel", "arbitrary")),
        name=name,
    )(x, w, res, g, b)


RG_TILE = 256
LOG2E = math.log2(math.e)
SQRT_FLOOR = 1e-30


def _rg_gates(uc, wg_ref, ba_ref, bx_ref, lam_ref, a_s, b_s):
    nsp2 = (-LRU_C * LOG2E) * _softplus(-lam_ref[...])
    for n in range(LRU_BLOCKS):
        sl = slice(n * LRU_BW, (n + 1) * LRU_BW)
        un = uc[:, sl]
        gates = jnp.dot(un.astype(BF16), wg_ref[n], preferred_element_type=F32)
        r = _sigmoid(gates[:, :LRU_BW] + ba_ref[:, sl])
        i = _sigmoid(gates[:, LRU_BW:] + bx_ref[:, sl])
        a = jnp.exp2(r * nsp2[:, sl])
        v = 1.0 - a * a
        a_s[:, sl] = a
        b_s[:, sl] = (v * lax.rsqrt(jnp.maximum(v, SQRT_FLOOR))) * (i * un)


def _rg_scan_tile(a_s, b_s, carry, *, reverse):
    sub = lax.broadcasted_iota(jnp.int32, (V7X_SUBLANES, D_MODEL), 0)
    n_groups = a_s.shape[0] // V7X_SUBLANES
    first = (sub == V7X_SUBLANES - 1) if reverse else (sub == 0)

    def group(gi, c):
        g = (n_groups - 1 - gi) if reverse else gi
        rows = pl.ds(pl.multiple_of(g * V7X_SUBLANES, V7X_SUBLANES), V7X_SUBLANES)
        a = a_s[rows, :]
        b = b_s[rows, :]
        b = b + jnp.where(first, a * c, 0.0)
        for s in (1, 2, 4):
            shift = (V7X_SUBLANES - s) if reverse else s
            m = (sub < V7X_SUBLANES - s) if reverse else (sub >= s)
            b = jnp.where(m, a * pltpu.roll(b, shift, 0) + b, b)
            if s < 4:
                a = jnp.where(m, a * pltpu.roll(a, shift, 0), a)
        a_s[rows, :] = b
        edge = b[0:1, :] if reverse else b[V7X_SUBLANES - 1:V7X_SUBLANES, :]
        return jnp.broadcast_to(edge, (V7X_SUBLANES, D_MODEL))

    carry[...] = lax.fori_loop(0, n_groups, group, carry[...])


def _rg_body(*refs, reverse):
    if reverse:
        uc_ref, wg_ref, ba_ref, bx_ref, lam_ref, hf_ref, gate_ref, o_ref, a_s, b_s, carry = refs
    else:
        uc_ref, wg_ref, ba_ref, bx_ref, lam_ref, o_ref, a_s, b_s, carry = refs

    @pl.when(pl.program_id(1) == 0)
    def _():
        carry[...] = jnp.zeros_like(carry)

    _rg_gates(uc_ref[0], wg_ref, ba_ref, bx_ref, lam_ref, a_s, b_s)
    _rg_scan_tile(a_s, b_s, carry, reverse=reverse)
    if reverse:
        o_ref[0] = ((hf_ref[0] + a_s[...]) * gate_ref[0].astype(F32)).astype(o_ref.dtype)
    else:
        o_ref[0] = a_s[...]


def _rg_core(uc, gate, p):
    bsz, seq, d = uc.shape
    tl = RG_TILE
    assert seq % tl == 0 and d == D_MODEL
    n_tiles = seq // tl
    row = pl.BlockSpec((1, d), lambda b, t: (0, 0))
    gate_w = pl.BlockSpec((LRU_BLOCKS, LRU_BW, 2 * LRU_BW), lambda b, t: (0, 0, 0))
    scan_scratch = [pltpu.VMEM((tl, d), F32), pltpu.VMEM((tl, d), F32), pltpu.VMEM((V7X_SUBLANES, d), F32)]

    fwd = pl.BlockSpec((1, tl, d), lambda b, t: (b, t, 0))
    hf = pl.pallas_call(
        functools.partial(_rg_body, reverse=False),
        grid=(bsz, n_tiles),
        in_specs=[fwd, gate_w, row, row, row],
        out_specs=fwd,
        out_shape=jax.ShapeDtypeStruct((bsz, seq, d), F32),
        scratch_shapes=scan_scratch,
        compiler_params=_params(("parallel", "arbitrary")),
        name="rg_fwd",
    )(uc, p["wg"][0], p["ba"][0], p["bx"][0], p["lam"][0])

    bwd = pl.BlockSpec((1, tl, d), lambda b, t: (b, n_tiles - 1 - t, 0))
    return pl.pallas_call(
        functools.partial(_rg_body, reverse=True),
        grid=(bsz, n_tiles),
        in_specs=[bwd, gate_w, row, row, row, bwd, bwd],
        out_specs=bwd,
        out_shape=jax.ShapeDtypeStruct((bsz, seq, d), BF16),
        scratch_shapes=scan_scratch,
        compiler_params=_params(("parallel", "arbitrary")),
        name="rg_bwd",
    )(uc, p["wg"][1], p["ba"][1], p["bx"][1], p["lam"][1], hf, gate)


def _prefix_rows(x, rid):
    q = x.shape[0]
    s = 1
    while s < q:
        x = x + jnp.where(rid >= s, pltpu.roll(x, s, 0), 0.0)
        s *= 2
    return x


def _suffix_rows(x, rid):
    q = x.shape[0]
    s = 1
    while s < q:
        x = x + jnp.where(rid < q - s, pltpu.roll(x, q - s, 0), 0.0)
        s *= 2
    return x


def _expand_heads(v, r_ref):
    hi = v.astype(BF16)
    lo = (v - hi.astype(F32)).astype(BF16)
    r = r_ref[...]
    return jnp.dot(hi, r, preferred_element_type=F32) + jnp.dot(lo, r, preferred_element_type=F32)


def _ssd_dt(dt_ref, dtb_ref, alog_ref):
    dt = _softplus(dt_ref[0] + dtb_ref[...])
    return dt, dt * (-jnp.exp(alog_ref[...]))


def _ssd1_body(x_ref, dt_ref, dtb_ref, alog_ref, dsk_ref, rf_ref, rb_ref, y1_ref, c_ref, sb_ref, hf_s):
    q = SSD_CHUNK

    @pl.when(pl.program_id(1) == 0)
    def _():
        hf_s[...] = jnp.zeros_like(hf_s)

    act = x_ref[0]
    xs = act[:, :SSD_D_INNER]
    bm = act[:, SSD_D_INNER:SSD_D_INNER + SSD_GROUPS * SSD_STATE]
    cm = act[:, SSD_D_INNER + SSD_GROUPS * SSD_STATE:]
    c_ref[0] = cm.astype(BF16)

    dt, dta = _ssd_dt(dt_ref, dtb_ref, alog_ref)
    rid = lax.broadcasted_iota(jnp.int32, (q, q), 0)
    cid = lax.broadcasted_iota(jnp.int32, (q, q), 1)
    is_fwd = cid < SSD_HEADS
    cum = jnp.where(is_fwd, _prefix_rows(dta, rid), _suffix_rows(dta, rid))
    tot = jnp.where(is_fwd[0:1, :], cum[q - 1:q, :], cum[0:1, :])
    e_in = jnp.exp(cum)
    w_end = dt * jnp.exp(tot - cum)

    ef_x = _expand_heads(e_in, rf_ref)
    xf = (xs * _expand_heads(w_end, rf_ref)).astype(BF16)
    xb = (xs * _expand_heads(w_end, rb_ref)).astype(BF16)

    dst = cum * LOG2E
    src_t = (dst - jnp.log2(dt)).T
    lower = rid >= cid
    upper = rid <= cid
    neg_inf = jnp.float32(-jnp.inf)
    lane_lo = cid < SSD_HEAD_DIM

    for g in range(SSD_GROUPS):
        ns = slice(g * SSD_STATE, (g + 1) * SSD_STATE)
        gs = slice(g * SSD_GROUP_W, (g + 1) * SSD_GROUP_W)
        bg = bm[:, ns]
        cg16 = cm[:, ns].astype(BF16)
        cb = lax.dot_general(cg16, bg.astype(BF16), (((1,), (1,)), ((), ())), preferred_element_type=F32)
        bgt16 = bg.T.astype(BF16)
        sf_t = jnp.dot(bgt16, xf[:, gs], preferred_element_type=F32)
        sb_ref[0, 0, :, gs] = jnp.dot(bgt16, xb[:, gs], preferred_element_type=F32).astype(sb_ref.dtype)
        h_in = hf_s[:, gs]
        y_off = jnp.dot(cg16, h_in.astype(BF16), preferred_element_type=F32) * ef_x[:, gs]
        hf_s[:, gs] = h_in * ef_x[q - 1:q, gs] + sf_t

        for pr in range(SSD_GROUP_W // (2 * SSD_HEAD_DIM)):
            h0 = g * (SSD_HEADS // SSD_GROUPS) + 2 * pr
            ms = []
            for h in (h0, h0 + 1):
                hb = SSD_HEADS + h
                d_f = dst[:, h:h + 1] - src_t[h:h + 1, :]
                d_b = dst[:, hb:hb + 1] - src_t[hb:hb + 1, :]
                m_f = jnp.exp2(jnp.where(lower, d_f, neg_inf))
                m_b = jnp.exp2(jnp.where(upper, d_b, neg_inf))
                ms.append((cb * (m_f + m_b)).astype(BF16))
            lhs = jnp.concatenate(ms, axis=1)
            ps = slice(h0 * SSD_HEAD_DIM, (h0 + 2) * SSD_HEAD_DIM)
            xp = xs[:, ps]
            rhs = jnp.concatenate([jnp.where(lane_lo, xp, 0.0).astype(BF16),
                                   jnp.where(lane_lo, 0.0, xp).astype(BF16)], axis=0)
            y_diag = jnp.dot(lhs, rhs, preferred_element_type=F32)
            os_ = slice(pr * 2 * SSD_HEAD_DIM, (pr + 1) * 2 * SSD_HEAD_DIM)
            y1_ref[0, :, ps] = (y_diag + y_off[:, os_] + xs[:, ps] * dsk_ref[:, ps]).astype(y1_ref.dtype)


def _ssd2_body(y1_ref, c_ref, sb_ref, dt_ref, zs_ref, dtb_ref, alog_ref, rb_ref, ng_ref, o_ref, hb_s):
    q = SSD_CHUNK
    c = pl.program_id(1)

    @pl.when(c == 0)
    def _():
        hb_s[...] = jnp.zeros_like(hb_s)

    _, dta = _ssd_dt(dt_ref, dtb_ref, alog_ref)
    rid = lax.broadcasted_iota(jnp.int32, (q, q), 0)
    eb_x = _expand_heads(jnp.exp(_suffix_rows(dta, rid)), rb_ref)

    for g in range(SSD_GROUPS):
        ns = slice(g * SSD_STATE, (g + 1) * SSD_STATE)
        gs = slice(g * SSD_GROUP_W, (g + 1) * SSD_GROUP_W)
        h_in = hb_s[:, gs]
        y_off = jnp.dot(c_ref[0, :, ns], h_in.astype(BF16), preferred_element_type=F32) * eb_x[:, gs]
        hb_s[:, gs] = h_in * eb_x[0:1, gs] + sb_ref[0, 0, :, gs].astype(F32)
        y = (y1_ref[0, :, gs].astype(F32) + y_off) * zs_ref[0, :, gs].astype(F32)
        ms = jnp.mean(y * y, axis=-1, keepdims=True)
        o_ref[0, :, gs] = (y * lax.rsqrt(ms + LN_EPS) * ng_ref[:, gs]).astype(BF16)


def _ssd_core(xbc_act, dt_raw, zs, dtb, alog, dskip_x, rf, rb, ng):
    bsz, seq, _ = xbc_act.shape
    q = SSD_CHUNK
    assert seq % q == 0 and q == 2 * SSD_HEADS
    nc = seq // q
    row = lambda r, w: pl.BlockSpec((r, w), lambda b, c: (0, 0))
    y1, c16, sb = pl.pallas_call(
        _ssd1_body,
        grid=(bsz, nc),
        in_specs=[
            pl.BlockSpec((1, q, SSD_CONV_DIM), lambda b, c: (b, c, 0)),
            pl.BlockSpec((1, q, 2 * SSD_HEADS), lambda b, c: (b, c, 0)),
            row(1, 2 * SSD_HEADS), row(1, 2 * SSD_HEADS),
            row(1, SSD_D_INNER), row(2 * SSD_HEADS, SSD_D_INNER), row(2 * SSD_HEADS, SSD_D_INNER),
        ],
        out_specs=[
            pl.BlockSpec((1, q, SSD_D_INNER), lambda b, c: (b, c, 0)),
            pl.BlockSpec((1, q, SSD_GROUPS * SSD_STATE), lambda b, c: (b, c, 0)),
            pl.BlockSpec((1, 1, SSD_STATE, SSD_D_INNER), lambda b, c: (b, c, 0, 0)),
        ],
        out_shape=[
            jax.ShapeDtypeStruct((bsz, seq, SSD_D_INNER), BF16),
            jax.ShapeDtypeStruct((bsz, seq, SSD_GROUPS * SSD_STATE), BF16),
            jax.ShapeDtypeStruct((bsz, nc, SSD_STATE, SSD_D_INNER), BF16),
        ],
        scratch_shapes=[pltpu.VMEM((SSD_STATE, SSD_D_INNER), F32)],
        compiler_params=_params(("parallel", "arbitrary")),
        name="ssd_fwd",
    )(xbc_act, dt_raw, dtb, alog, dskip_x, rf, rb)

    rev = lambda b, c: (b, nc - 1 - c, 0)
    return pl.pallas_call(
        _ssd2_body,
        grid=(bsz, nc),
        in_specs=[
            pl.BlockSpec((1, q, SSD_D_INNER), rev),
            pl.BlockSpec((1, q, SSD_GROUPS * SSD_STATE), rev),
            pl.BlockSpec((1, 1, SSD_STATE, SSD_D_INNER), lambda b, c: (b, nc - 1 - c, 0, 0)),
            pl.BlockSpec((1, q, 2 * SSD_HEADS), rev),
            pl.BlockSpec((1, q, SSD_D_INNER), rev),
            row(1, 2 * SSD_HEADS), row(1, 2 * SSD_HEADS), row(2 * SSD_HEADS, SSD_D_INNER), row(1, SSD_D_INNER),
        ],
        out_specs=pl.BlockSpec((1, q, SSD_D_INNER), rev),
        out_shape=jax.ShapeDtypeStruct((bsz, seq, SSD_D_INNER), BF16),
        scratch_shapes=[pltpu.VMEM((SSD_STATE, SSD_D_INNER), F32)],
        compiler_params=_params(("parallel", "arbitrary")),
        name="ssd_bwd",
    )(y1, c16, sb, dt_raw, zs, dtb, alog, rb, ng)


NA_KEYS = WIN_H * GRID_W


NA_ROWS_PER_STEP = 16


def _na_body(q_ref, k_ref, v_ref, bias_ref, o_ref, *, rows):
    scale = NA_HEAD_DIM ** -0.5

    def step(i, carry):
        qss, kss, scores, probs = [], [], [], []
        for j in range(NA_ROWS_PER_STEP):
            r = i * NA_ROWS_PER_STEP + j
            rs = jnp.clip(r - WIN_H // 2, 0, rows - WIN_H)
            dy0 = rs - r + (WIN_H - 1)
            qs = pl.ds(pl.multiple_of(r * GRID_W, GRID_W), GRID_W)
            ks = pl.ds(pl.multiple_of(rs * GRID_W, GRID_W), NA_KEYS)
            s = lax.dot_general(q_ref[0, qs, :], k_ref[0, ks, :], (((1,), (1,)), ((), ())),
                                preferred_element_type=F32)
            scores.append(s * scale + bias_ref[0, dy0])
            qss.append(qs)
            kss.append(ks)
        for s in scores:
            p = jnp.exp(s - jnp.max(s, axis=-1, keepdims=True))
            probs.append((p.astype(BF16), 1.0 / jnp.sum(p, axis=-1, keepdims=True)))
        for qs, ks, (p, inv) in zip(qss, kss, probs):
            o = jnp.dot(p, v_ref[0, ks, :], preferred_element_type=F32)
            o_ref[0, qs, :] = (o * inv).astype(o_ref.dtype)
        return carry

    lax.fori_loop(0, rows // NA_ROWS_PER_STEP, step, 0)


def _na_bias_table(rpb):
    qc = np.arange(GRID_W)[:, None]
    kc = np.arange(GRID_W)[None, :]
    wstart = np.clip(qc - WIN_W // 2, 0, GRID_W - WIN_W)
    ok = (kc >= wstart) & (kc < wstart + WIN_W)
    dx = np.clip(kc - qc + WIN_W - 1, 0, 2 * WIN_W - 2)
    per_dy = jnp.where(ok[None, None], rpb[:, :, dx].astype(F32), -jnp.inf)
    dy = np.arange(WIN_H)[:, None] + np.arange(WIN_H)[None, :]
    tab = per_dy[:, dy]
    return jnp.transpose(tab, (0, 1, 3, 2, 4)).reshape(NA_HEADS, WIN_H, GRID_W, NA_KEYS)


def _natten(qkv, bias_tab):
    bsz, seq, _ = qkv.shape
    rows = seq // GRID_W
    assert seq % GRID_W == 0 and rows >= WIN_H and rows % NA_ROWS_PER_STEP == 0
    blk = lambda off: pl.BlockSpec((1, seq, NA_HEAD_DIM), lambda b, h: (b, 0, off + h))
    return pl.pallas_call(
        functools.partial(_na_body, rows=rows),
        grid=(bsz, NA_HEADS),
        in_specs=[blk(0), blk(NA_HEADS), blk(2 * NA_HEADS),
                  pl.BlockSpec((1, WIN_H, GRID_W, NA_KEYS), lambda b, h: (h, 0, 0, 0))],
        out_specs=blk(0),
        out_shape=jax.ShapeDtypeStruct((bsz, seq, D_MODEL), BF16),
        compiler_params=_params(("parallel", "parallel")),
        name="natten",
    )(qkv, qkv, qkv, bias_tab)


def _row(v):
    return v.reshape(1, -1).astype(F32)


def _rg_layer(x, shape, p):
    bsz, seq = shape
    gate = _matmul(x, p["w_gate"], act="gelu", out_dtype=BF16, bn=2048, name="rg_in_gate")
    uc = _matmul_conv(x.reshape(bsz, seq, D_MODEL), p["w_u"], p["cw"], p["cb"], name="rg_in_u_conv")
    y = _rg_core(uc, gate.reshape(bsz, seq, D_MODEL), p)
    return y.reshape(bsz * seq, D_MODEL), p["w_out"]


def _ssd_layer(x, shape, p):
    bsz, seq = shape
    zs = _matmul(x, p["w_z"], act="silu", out_dtype=BF16, bn=2048, name="ssd_in_z").reshape(bsz, seq, SSD_D_INNER)
    xbc = _matmul_conv(x.reshape(bsz, seq, D_MODEL), p["w_xbc"], p["cw"], p["cb"], act="silu", name="ssd_in_xbc_conv")
    dt = _matmul(x, p["w_dt"], bn=2 * SSD_HEADS, name="ssd_in_dt").reshape(bsz, seq, 2 * SSD_HEADS)
    y = _ssd_core(xbc, dt, zs, p["dtb"], p["alog"], p["dskip_x"], p["rf"], p["rb"], p["ng"])
    return y.reshape(bsz * seq, SSD_D_INNER), p["w_out"]


def _na_layer(x, shape, p):
    bsz, seq = shape
    qkv = _matmul(x, p["w_qkv"], p["b_qkv"], out_dtype=BF16, bn=2048, name="na_qkv").reshape(bsz, seq, 3 * D_MODEL)
    o = _natten(qkv, p["bias_tab"])
    return o.reshape(bsz * seq, D_MODEL), p["w_out"]


def _head_expand_matrix(offset):
    r = np.zeros((2 * SSD_HEADS, SSD_D_INNER), np.float32)
    for h in range(SSD_HEADS):
        r[offset + h, h * SSD_HEAD_DIM:(h + 1) * SSD_HEAD_DIM] = 1.0
    return jnp.asarray(r, BF16)


def _rg_params(w_in, conv_w, conv_b, w_a, b_a, w_x, b_x, lam, w_out):
    return dict(w_gate=w_in[:, :D_MODEL].astype(BF16), w_u=w_in[:, D_MODEL:].astype(BF16),
                cw=conv_w, cb=_row(conv_b),
                wg=jnp.concatenate([w_a, w_x], axis=-1).astype(BF16),
                ba=b_a.reshape(2, 1, D_MODEL), bx=b_x.reshape(2, 1, D_MODEL), lam=lam.reshape(2, 1, D_MODEL),
                w_out=w_out.astype(BF16))


def _ssd_params(w_in, conv_w, conv_b, dt_bias, a_log, d_skip, norm_g, w_out):
    return dict(w_z=w_in[:, :SSD_D_INNER].astype(BF16),
                w_xbc=w_in[:, SSD_D_INNER:SSD_D_INNER + SSD_CONV_DIM].astype(BF16),
                w_dt=w_in[:, SSD_D_INNER + SSD_CONV_DIM:].astype(BF16),
                cw=conv_w, cb=_row(conv_b), dtb=_row(dt_bias), alog=_row(a_log),
                dskip_x=_row(jnp.repeat(d_skip, SSD_HEAD_DIM)),
                rf=_head_expand_matrix(0), rb=_head_expand_matrix(SSD_HEADS),
                ng=_row(norm_g), w_out=w_out.astype(BF16))


def _na_params(w_qkv, b_qkv, rpb, w_out):
    return dict(w_qkv=w_qkv.astype(BF16), b_qkv=_row(b_qkv), bias_tab=_na_bias_table(rpb),
                w_out=w_out.astype(BF16))


def kernel(x_prompt, x_sample, rg_w_in, rg_conv_w, rg_conv_b, rg_w_a, rg_b_a, rg_w_x, rg_b_x, rg_lambda, rg_w_out,
           ssd_w_in, ssd_conv_w, ssd_conv_b, ssd_dt_bias, ssd_a_log, ssd_d, ssd_norm_g, ssd_w_out,
           na_w_qkv, na_b_qkv, na_rpb, na_w_out, mlp_w_up, mlp_w_down, ln1_g, ln1_b, ln2_g, ln2_b):
    layers = []
    for i in range(DEPTH):
        kind, j = i % 3, i // 3
        if kind == 0:
            fn, p = _rg_layer, _rg_params(rg_w_in[j], rg_conv_w[j], rg_conv_b[j], rg_w_a[j], rg_b_a[j], rg_w_x[j],
                                          rg_b_x[j], rg_lambda[j], rg_w_out[j])
        elif kind == 1:
            fn, p = _ssd_layer, _ssd_params(ssd_w_in[j], ssd_conv_w[j], ssd_conv_b[j], ssd_dt_bias[j], ssd_a_log[j],
                                            ssd_d[j], ssd_norm_g[j], ssd_w_out[j])
        else:
            fn, p = _na_layer, _na_params(na_w_qkv[j], na_b_qkv[j], na_rpb[j], na_w_out[j])
        layers.append((fn, p, mlp_w_up[i].astype(BF16), mlp_w_down[i].astype(BF16),
                       _row(ln1_g[i]), _row(ln1_b[i]), _row(ln2_g[i]), _row(ln2_b[i])))

    def trunk(x):
        bsz, seq, _ = x.shape
        x = x.reshape(bsz * seq, D_MODEL)
        for fn, p, w_up, w_down, g1, b1, g2, b2 in layers:
            y, w_out = fn(x, (bsz, seq), p)
            x = _matmul_ln(y, w_out, x, g1, b1, name="mixer_out_ln")
            hid = _matmul(x, w_up, act="relu2", out_dtype=BF16, bn=2048, name="mlp_up")
            x = _matmul_ln(hid, w_down, x, g2, b2, name="mlp_down_ln")
        return x.reshape(bsz, seq, D_MODEL)

    return trunk(x_prompt), trunk(x_sample)
```

```python
import functools
import math

import numpy as np
import jax
import jax.numpy as jnp
from jax import lax
from jax.experimental import pallas as pl
from jax.experimental.pallas import tpu as pltpu

F32 = jnp.float32
BF16 = jnp.bfloat16

D_MODEL = 2048
DEPTH = 4
ALPHA = (2 * DEPTH) ** 0.25
LN_EPS = 1e-5
CONV_W = 4
LRU_BLOCKS = 8
LRU_BW = D_MODEL // LRU_BLOCKS
LRU_C = 8.0
SSD_D_INNER = 2 * D_MODEL
SSD_HEAD_DIM = 64
SSD_HEADS = SSD_D_INNER // SSD_HEAD_DIM
SSD_GROUPS = 8
SSD_STATE = 128
SSD_CHUNK = 128
SSD_GROUP_W = SSD_D_INNER // SSD_GROUPS
SSD_CONV_DIM = SSD_D_INNER + 2 * SSD_GROUPS * SSD_STATE
GRID_W = 64
WIN_H = 8
WIN_W = 16
NA_HEAD_DIM = 128
NA_HEADS = D_MODEL // NA_HEAD_DIM
MLP_HIDDEN = 4 * D_MODEL

V7X_SUBLANES = 8
V7X_LANES = 128
V7X_VMEM_LIMIT_BYTES = 56 * 1024 * 1024


def _params(semantics, vmem_bytes=V7X_VMEM_LIMIT_BYTES):
    return pltpu.CompilerParams(dimension_semantics=semantics, vmem_limit_bytes=vmem_bytes)


def _sigmoid(x):
    return 1.0 / (1.0 + jnp.exp(-x))


def _softplus(x):
    return jnp.maximum(x, 0.0) + jnp.log1p(jnp.exp(-jnp.abs(x)))


def _gelu_tanh(x):
    c = math.sqrt(2.0 / math.pi)
    return 0.5 * x * (1.0 + jnp.tanh(c * (x + 0.044715 * (x * x * x))))


def _relu2(x):
    r = jnp.maximum(x, 0.0)
    return r * r


def _silu(x):
    return x * _sigmoid(x)


_ACTS = {"none": lambda v: v, "gelu": _gelu_tanh, "relu2": _relu2, "silu": _silu}


def _mm_body(*refs, act, has_bias):
    if has_bias:
        x_ref, w_ref, b_ref, o_ref = refs
    else:
        x_ref, w_ref, o_ref = refs
    acc = jnp.dot(x_ref[...].astype(BF16), w_ref[...], preferred_element_type=F32)
    if has_bias:
        acc = acc + b_ref[...]
    o_ref[...] = _ACTS[act](acc).astype(o_ref.dtype)


def _matmul(x, w, bias=None, *, act="none", out_dtype=F32, bm=1024, bn=1024, name="mm"):
    m, k = x.shape
    k2, n = w.shape
    assert k == k2 and m % bm == 0 and n % bn == 0, (x.shape, w.shape, bm, bn)
    in_specs = [pl.BlockSpec((bm, k), lambda i, j: (i, 0)),
                pl.BlockSpec((k, bn), lambda i, j: (0, j))]
    args = [x, w]
    if bias is not None:
        in_specs.append(pl.BlockSpec((1, bn), lambda i, j: (0, j)))
        args.append(bias)
    return pl.pallas_call(
        functools.partial(_mm_body, act=act, has_bias=bias is not None),
        grid=(m // bm, n // bn),
        in_specs=in_specs,
        out_specs=pl.BlockSpec((bm, bn), lambda i, j: (i, j)),
        out_shape=jax.ShapeDtypeStruct((m, n), out_dtype),
        compiler_params=_params(("parallel", "parallel")),
        name=name,
    )(*args)


CONV_HALO = 16
CONV_SUBDOT = 256


def _mm_conv_body(x_ref, xp_ref, xn_ref, w_ref, cw_ref, cb_ref, o_ref, *, act, n_tiles):
    t = pl.program_id(1)
    rows = x_ref.shape[1]
    n = rows + 2 * CONV_HALO
    lo, hi = CONV_HALO, CONV_HALO + rows
    ext = jnp.concatenate([xp_ref[0] * (t > 0).astype(xp_ref.dtype), x_ref[0],
                           xn_ref[0] * (t < n_tiles - 1).astype(xn_ref.dtype)], axis=0).astype(BF16)
    for c in range(o_ref.shape[2] // CONV_SUBDOT):
        cols = slice(c * CONV_SUBDOT, (c + 1) * CONV_SUBDOT)
        y = jnp.dot(ext, w_ref[:, cols], preferred_element_type=F32)
        v = (cb_ref[:, cols] + pltpu.roll(y, 1, 0)[lo:hi] * cw_ref[0:1, cols] + y[lo:hi] * cw_ref[1:2, cols]
             + pltpu.roll(y, n - 1, 0)[lo:hi] * cw_ref[2:3, cols] + pltpu.roll(y, n - 2, 0)[lo:hi] * cw_ref[3:4, cols])
        o_ref[0, :, cols] = _ACTS[act](v).astype(o_ref.dtype)


def _matmul_conv(x, w, cw, cb, *, act="none", tl=1024, bn=1024, name="mm_conv"):
    bsz, seq, k = x.shape
    n = w.shape[1]
    assert seq % tl == 0 and n % bn == 0 and bn % CONV_SUBDOT == 0 and tl % CONV_HALO == 0
    n_tiles = seq // tl
    hb = tl // CONV_HALO
    n_hblk = seq // CONV_HALO
    return pl.pallas_call(
        functools.partial(_mm_conv_body, act=act, n_tiles=n_tiles),
        grid=(bsz, n_tiles, n // bn),
        in_specs=[pl.BlockSpec((1, tl, k), lambda b, t, j: (b, t, 0)),
                  pl.BlockSpec((1, CONV_HALO, k), lambda b, t, j: (b, jnp.maximum(t * hb - 1, 0), 0)),
                  pl.BlockSpec((1, CONV_HALO, k), lambda b, t, j: (b, jnp.minimum((t + 1) * hb, n_hblk - 1), 0)),
                  pl.BlockSpec((k, bn), lambda b, t, j: (0, j)),
                  pl.BlockSpec((CONV_W, bn), lambda b, t, j: (0, j)),
                  pl.BlockSpec((1, bn), lambda b, t, j: (0, j))],
        out_specs=pl.BlockSpec((1, tl, bn), lambda b, t, j: (b, t, j)),
        out_shape=jax.ShapeDtypeStruct((bsz, seq, n), F32),
        compiler_params=_params(("parallel", "parallel", "parallel")),
        name=name,
    )(x, x, x, w, cw, cb)


LN_ROW_SPLIT = 4


def _mm_ln_body(x_ref, w_ref, res_ref, g_ref, b_ref, o_ref, *, nk):
    def dot(rows):
        return jnp.dot(x_ref[rows, :], w_ref[...], preferred_element_type=F32)

    def finish():
        hm = x_ref.shape[0] // LN_ROW_SPLIT
        for h in range(LN_ROW_SPLIT):
            rows = slice(h * hm, (h + 1) * hm)
            z = ALPHA * res_ref[rows, :] + dot(rows)
            if nk > 1:
                z = z + o_ref[rows, :]
            mu = jnp.mean(z, axis=-1, keepdims=True)
            zc = z - mu
            var = jnp.mean(zc * zc, axis=-1, keepdims=True)
            o_ref[rows, :] = zc * lax.rsqrt(var + LN_EPS) * g_ref[...] + b_ref[...]

    if nk == 1:
        finish()
        return
    kk = pl.program_id(1)
    every = slice(None)

    @pl.when(kk == 0)
    def _():
        o_ref[...] = dot(every)

    if nk > 2:
        @pl.when(jnp.logical_and(kk > 0, kk < nk - 1))
        def _():
            o_ref[...] += dot(every)

    pl.when(kk == nk - 1)(finish)


def _matmul_ln(x, w, res, g, b, *, bm=1024, bk=1024, name="mm_ln"):
    m, k = x.shape
    k2, n = w.shape
    assert k == k2 and n == D_MODEL and m % bm == 0 and k % bk == 0 and bm % LN_ROW_SPLIT == 0
    nk = k // bk
    return pl.pallas_call(
        functools.partial(_mm_ln_body, nk=nk),
        grid=(m // bm, nk),
        in_specs=[pl.BlockSpec((bm, bk), lambda i, kk: (i, kk)),
                  pl.BlockSpec((bk, n), lambda i, kk: (kk, 0)),
                  pl.BlockSpec((bm, n), lambda i, kk: (i, 0)),
                  pl.BlockSpec((1, n), lambda i, kk: (0, 0)),
                  pl.BlockSpec((1, n), lambda i, kk: (0, 0))],
        out_specs=pl.BlockSpec((bm, n), lambda i, kk: (i, 0)),
        out_shape=jax.ShapeDtypeStruct((m, n), F32),
        compiler_params=_params(("parallel", "arbitrary")),
        name=name,
    )(x, w, res, g, b)


RG_TILE = 512
RG_CHUNKS = V7X_SUBLANES
RG_CHUNK_ROWS = RG_TILE // RG_CHUNKS
RG_SLABS = D_MODEL // V7X_LANES
LOG2E = math.log2(math.e)
SQRT_FLOOR = 1e-30


def _sigmoid_tanh(x):
    return 0.5 * jnp.tanh(0.5 * x) + 0.5


def _slab_rows(slab, chunk):
    return pl.ds(slab * RG_TILE + chunk, RG_CHUNK_ROWS, stride=RG_CHUNKS)


def _rg_gates(uc, wg_ref, ba_ref, bx_ref, lam_ref, a_s, b_s):
    nsp2 = (-LRU_C * LOG2E) * _softplus(-lam_ref[...])
    for n in range(LRU_BLOCKS):
        sl = slice(n * LRU_BW, (n + 1) * LRU_BW)
        un = uc[:, sl]
        gates = jnp.dot(un.astype(BF16), wg_ref[n], preferred_element_type=F32)
        r = _sigmoid_tanh(gates[:, :LRU_BW] + ba_ref[:, sl])
        i = _sigmoid_tanh(gates[:, LRU_BW:] + bx_ref[:, sl])
        a = jnp.exp2(r * nsp2[:, sl])
        v = 1.0 - a * a
        b = (v * lax.rsqrt(jnp.maximum(v, SQRT_FLOOR))) * (i * un)
        for half in range(LRU_BW // V7X_LANES):
            slab = n * (LRU_BW // V7X_LANES) + half
            lanes = slice(half * V7X_LANES, (half + 1) * V7X_LANES)
            for c in range(RG_CHUNKS):
                rows = slice(c * RG_CHUNK_ROWS, (c + 1) * RG_CHUNK_ROWS)
                a_s[_slab_rows(slab, c), :] = a[rows, lanes]
                b_s[_slab_rows(slab, c), :] = b[rows, lanes]


def _rg_scan_tile(a_s, b_s, carry, *, reverse):
    group = RG_SLABS * V7X_SUBLANES

    def step(i, hp):
        h, p = hp
        t = (RG_CHUNK_ROWS - 1 - i) if reverse else i
        rows = [pl.ds(pl.multiple_of(slab * RG_TILE + t * V7X_SUBLANES, V7X_SUBLANES), V7X_SUBLANES)
                for slab in range(RG_SLABS)]
        a = jnp.concatenate([a_s[r, :] for r in rows], axis=0)
        b = jnp.concatenate([b_s[r, :] for r in rows], axis=0)
        h = a * h + b
        p = p * a
        for slab, r in enumerate(rows):
            a_s[r, :] = h[slab * V7X_SUBLANES:(slab + 1) * V7X_SUBLANES]
            b_s[r, :] = p[slab * V7X_SUBLANES:(slab + 1) * V7X_SUBLANES]
        return h, p

    zeros = jnp.zeros((group, V7X_LANES), F32)
    q, p = lax.fori_loop(0, RG_CHUNK_ROWS, step, (zeros, zeros + 1.0))

    sub = lax.broadcasted_iota(jnp.int32, (group, V7X_LANES), 0) % V7X_SUBLANES
    for s in (1, 2, 4):
        shift = (group - s) if reverse else s
        m = (sub < V7X_SUBLANES - s) if reverse else (sub >= s)
        q = jnp.where(m, p * pltpu.roll(q, shift, 0) + q, q)
        p = jnp.where(m, p * pltpu.roll(p, shift, 0), p)
    c_in = carry[...]
    after = p * c_in + q
    first = (sub == V7X_SUBLANES - 1) if reverse else (sub == 0)
    entering = jnp.where(first, c_in, pltpu.roll(after, (group - 1) if reverse else 1, 0))
    last = 0 if reverse else V7X_SUBLANES - 1
    carry[...] = jnp.concatenate(
        [jnp.broadcast_to(after[slab * V7X_SUBLANES + last:slab * V7X_SUBLANES + last + 1, :],
                          (V7X_SUBLANES, V7X_LANES)) for slab in range(RG_SLABS)], axis=0)
    return entering


def _rg_body(*refs, reverse):
    if reverse:
        uc_ref, wg_ref, ba_ref, bx_ref, lam_ref, hf_ref, gate_ref, o_ref, a_s, b_s, carry = refs
    else:
        uc_ref, wg_ref, ba_ref, bx_ref, lam_ref, o_ref, a_s, b_s, carry = refs

    @pl.when(pl.program_id(1) == 0)
    def _():
        carry[...] = jnp.zeros_like(carry)

    _rg_gates(uc_ref[0], wg_ref, ba_ref, bx_ref, lam_ref, a_s, b_s)
    entering = _rg_scan_tile(a_s, b_s, carry, reverse=reverse)
    for slab in range(RG_SLABS):
        lanes = slice(slab * V7X_LANES, (slab + 1) * V7X_LANES)
        for c in range(RG_CHUNKS):
            rows = slice(c * RG_CHUNK_ROWS, (c + 1) * RG_CHUNK_ROWS)
            e = entering[slab * V7X_SUBLANES + c:slab * V7X_SUBLANES + c + 1, :]
            h = a_s[_slab_rows(slab, c), :] + b_s[_slab_rows(slab, c), :] * e
            if reverse:
                o_ref[0, rows, lanes] = ((hf_ref[0, rows, lanes] + h)
                                         * gate_ref[0, rows, lanes].astype(F32)).astype(o_ref.dtype)
            else:
                o_ref[0, rows, lanes] = h


def _rg_core(uc, gate, p):
    bsz, seq, d = uc.shape
    tl = RG_TILE
    assert seq % tl == 0 and d == D_MODEL
    n_tiles = seq // tl
    row = pl.BlockSpec((1, d), lambda b, t: (0, 0))
    gate_w = pl.BlockSpec((LRU_BLOCKS, LRU_BW, 2 * LRU_BW), lambda b, t: (0, 0, 0))
    scan_scratch = [pltpu.VMEM((RG_SLABS * tl, V7X_LANES), F32), pltpu.VMEM((RG_SLABS * tl, V7X_LANES), F32),
                    pltpu.VMEM((RG_SLABS * V7X_SUBLANES, V7X_LANES), F32)]

    fwd = pl.BlockSpec((1, tl, d), lambda b, t: (b, t, 0))
    hf = pl.pallas_call(
        functools.partial(_rg_body, reverse=False),
        grid=(bsz, n_tiles),
        in_specs=[fwd, gate_w, row, row, row],
        out_specs=fwd,
        out_shape=jax.ShapeDtypeStruct((bsz, seq, d), F32),
        scratch_shapes=scan_scratch,
        compiler_params=_params(("parallel", "arbitrary")),
        name="rg_fwd",
    )(uc, p["wg"][0], p["ba"][0], p["bx"][0], p["lam"][0])

    bwd = pl.BlockSpec((1, tl, d), lambda b, t: (b, n_tiles - 1 - t, 0))
    return pl.pallas_call(
        functools.partial(_rg_body, reverse=True),
        grid=(bsz, n_tiles),
        in_specs=[bwd, gate_w, row, row, row, bwd, bwd],
        out_specs=bwd,
        out_shape=jax.ShapeDtypeStruct((bsz, seq, d), BF16),
        scratch_shapes=scan_scratch,
        compiler_params=_params(("parallel", "arbitrary")),
        name="rg_bwd",
    )(uc, p["wg"][1], p["ba"][1], p["bx"][1], p["lam"][1], hf, gate)


def _prefix_rows(x, rid):
    q = x.shape[0]
    s = 1
    while s < q:
        x = x + jnp.where(rid >= s, pltpu.roll(x, s, 0), 0.0)
        s *= 2
    return x


def _suffix_rows(x, rid):
    q = x.shape[0]
    s = 1
    while s < q:
        x = x + jnp.where(rid < q - s, pltpu.roll(x, q - s, 0), 0.0)
        s *= 2
    return x


def _expand_heads(v, r_ref, *, split=True):
    hi = v.astype(BF16)
    out = jnp.dot(hi, r_ref[...], preferred_element_type=F32)
    if split:
        out = out + jnp.dot((v - hi.astype(F32)).astype(BF16), r_ref[...], preferred_element_type=F32)
    return out


def _ssd_dt(dt_ref, dtb_ref, alog_ref):
    dt = _softplus(dt_ref[0] + dtb_ref[...])
    return dt, dt * (-jnp.exp(alog_ref[...]))


def _ssd1_body(x_ref, dt_ref, dtb_ref, alog_ref, dsk_ref, rf_ref, rb_ref, y1_ref, c_ref, sb_ref, hf_s):
    q = SSD_CHUNK

    @pl.when(pl.program_id(1) == 0)
    def _():
        hf_s[...] = jnp.zeros_like(hf_s)

    act = x_ref[0]
    xs = act[:, :SSD_D_INNER]
    bm = act[:, SSD_D_INNER:SSD_D_INNER + SSD_GROUPS * SSD_STATE]
    cm = act[:, SSD_D_INNER + SSD_GROUPS * SSD_STATE:]
    c_ref[0] = cm.astype(BF16)

    dt, dta = _ssd_dt(dt_ref, dtb_ref, alog_ref)
    rid = lax.broadcasted_iota(jnp.int32, (q, q), 0)
    cid = lax.broadcasted_iota(jnp.int32, (q, q), 1)
    is_fwd = cid < SSD_HEADS
    cum = jnp.where(is_fwd, _prefix_rows(dta, rid), _suffix_rows(dta, rid))
    tot = jnp.where(is_fwd[0:1, :], cum[q - 1:q, :], cum[0:1, :])
    e_in = jnp.exp(cum)
    w_end = dt * jnp.exp(tot - cum)

    ef_x = _expand_heads(e_in, rf_ref)
    xf = (xs * _expand_heads(w_end, rf_ref, split=False)).astype(BF16)
    xb = (xs * _expand_heads(w_end, rb_ref, split=False)).astype(BF16)

    dst = cum * LOG2E
    src_t = (dst - jnp.log2(dt)).T
    lower = rid >= cid
    upper = rid <= cid
    neg_inf = jnp.float32(-jnp.inf)
    lane_lo = cid < SSD_HEAD_DIM

    for g in range(SSD_GROUPS):
        ns = slice(g * SSD_STATE, (g + 1) * SSD_STATE)
        gs = slice(g * SSD_GROUP_W, (g + 1) * SSD_GROUP_W)
        bg = bm[:, ns]
        cg16 = cm[:, ns].astype(BF16)
        cb = lax.dot_general(cg16, bg.astype(BF16), (((1,), (1,)), ((), ())), preferred_element_type=F32)
        bgt16 = bg.T.astype(BF16)
        sf_t = jnp.dot(bgt16, xf[:, gs], preferred_element_type=F32)
        sb_ref[0, 0, :, gs] = jnp.dot(bgt16, xb[:, gs], preferred_element_type=F32).astype(sb_ref.dtype)
        h_in = hf_s[:, gs]
        y_off = jnp.dot(cg16, h_in.astype(BF16), preferred_element_type=F32) * ef_x[:, gs]
        hf_s[:, gs] = h_in * ef_x[q - 1:q, gs] + sf_t

        for pr in range(SSD_GROUP_W // (2 * SSD_HEAD_DIM)):
            h0 = g * (SSD_HEADS // SSD_GROUPS) + 2 * pr
            ms = []
            for h in (h0, h0 + 1):
                hb = SSD_HEADS + h
                d_f = dst[:, h:h + 1] - src_t[h:h + 1, :]
                d_b = dst[:, hb:hb + 1] - src_t[hb:hb + 1, :]
                m_f = jnp.exp2(jnp.where(lower, d_f, neg_inf))
                m_b = jnp.exp2(jnp.where(upper, d_b, neg_inf))
                ms.append((cb * (m_f + m_b)).astype(BF16))
            lhs = jnp.concatenate(ms, axis=1)
            ps = slice(h0 * SSD_HEAD_DIM, (h0 + 2) * SSD_HEAD_DIM)
            xp = xs[:, ps]
            rhs = jnp.concatenate([jnp.where(lane_lo, xp, 0.0).astype(BF16),
                                   jnp.where(lane_lo, 0.0, xp).astype(BF16)], axis=0)
            y_diag = jnp.dot(lhs, rhs, preferred_element_type=F32)
            os_ = slice(pr * 2 * SSD_HEAD_DIM, (pr + 1) * 2 * SSD_HEAD_DIM)
            y1_ref[0, :, ps] = (y_diag + y_off[:, os_] + xs[:, ps] * dsk_ref[:, ps]).astype(y1_ref.dtype)


def _ssd2_body(y1_ref, c_ref, sb_ref, dt_ref, zs_ref, dtb_ref, alog_ref, rb_ref, ng_ref, o_ref, hb_s):
    q = SSD_CHUNK
    c = pl.program_id(1)

    @pl.when(c == 0)
    def _():
        hb_s[...] = jnp.zeros_like(hb_s)

    _, dta = _ssd_dt(dt_ref, dtb_ref, alog_ref)
    rid = lax.broadcasted_iota(jnp.int32, (q, q), 0)
    eb_x = _expand_heads(jnp.exp(_suffix_rows(dta, rid)), rb_ref)

    for g in range(SSD_GROUPS):
        ns = slice(g * SSD_STATE, (g + 1) * SSD_STATE)
        gs = slice(g * SSD_GROUP_W, (g + 1) * SSD_GROUP_W)
        h_in = hb_s[:, gs]
        y_off = jnp.dot(c_ref[0, :, ns], h_in.astype(BF16), preferred_element_type=F32) * eb_x[:, gs]
        hb_s[:, gs] = h_in * eb_x[0:1, gs] + sb_ref[0, 0, :, gs].astype(F32)
        y = (y1_ref[0, :, gs].astype(F32) + y_off) * zs_ref[0, :, gs].astype(F32)
        ms = jnp.mean(y * y, axis=-1, keepdims=True)
        o_ref[0, :, gs] = (y * lax.rsqrt(ms + LN_EPS) * ng_ref[:, gs]).astype(BF16)


def _ssd_core(xbc_act, dt_raw, zs, dtb, alog, dskip_x, rf, rb, ng):
    bsz, seq, _ = xbc_act.shape
    q = SSD_CHUNK
    assert seq % q == 0 and q == 2 * SSD_HEADS
    nc = seq // q
    row = lambda r, w: pl.BlockSpec((r, w), lambda b, c: (0, 0))
    y1, c16, sb = pl.pallas_call(
        _ssd1_body,
        grid=(bsz, nc),
        in_specs=[
            pl.BlockSpec((1, q, SSD_CONV_DIM), lambda b, c: (b, c, 0)),
            pl.BlockSpec((1, q, 2 * SSD_HEADS), lambda b, c: (b, c, 0)),
            row(1, 2 * SSD_HEADS), row(1, 2 * SSD_HEADS),
            row(1, SSD_D_INNER), row(2 * SSD_HEADS, SSD_D_INNER), row(2 * SSD_HEADS, SSD_D_INNER),
        ],
        out_specs=[
            pl.BlockSpec((1, q, SSD_D_INNER), lambda b, c: (b, c, 0)),
            pl.BlockSpec((1, q, SSD_GROUPS * SSD_STATE), lambda b, c: (b, c, 0)),
            pl.BlockSpec((1, 1, SSD_STATE, SSD_D_INNER), lambda b, c: (b, c, 0, 0)),
        ],
        out_shape=[
            jax.ShapeDtypeStruct((bsz, seq, SSD_D_INNER), BF16),
            jax.ShapeDtypeStruct((bsz, seq, SSD_GROUPS * SSD_STATE), BF16),
            jax.ShapeDtypeStruct((bsz, nc, SSD_STATE, SSD_D_INNER), BF16),
        ],
        scratch_shapes=[pltpu.VMEM((SSD_STATE, SSD_D_INNER), F32)],
        compiler_params=_params(("parallel", "arbitrary")),
        name="ssd_fwd",
    )(xbc_act, dt_raw, dtb, alog, dskip_x, rf, rb)

    rev = lambda b, c: (b, nc - 1 - c, 0)
    return pl.pallas_call(
        _ssd2_body,
        grid=(bsz, nc),
        in_specs=[
            pl.BlockSpec((1, q, SSD_D_INNER), rev),
            pl.BlockSpec((1, q, SSD_GROUPS * SSD_STATE), rev),
            pl.BlockSpec((1, 1, SSD_STATE, SSD_D_INNER), lambda b, c: (b, nc - 1 - c, 0, 0)),
            pl.BlockSpec((1, q, 2 * SSD_HEADS), rev),
            pl.BlockSpec((1, q, SSD_D_INNER), rev),
            row(1, 2 * SSD_HEADS), row(1, 2 * SSD_HEADS), row(2 * SSD_HEADS, SSD_D_INNER), row(1, SSD_D_INNER),
        ],
        out_specs=pl.BlockSpec((1, q, SSD_D_INNER), rev),
        out_shape=jax.ShapeDtypeStruct((bsz, seq, SSD_D_INNER), BF16),
        scratch_shapes=[pltpu.VMEM((SSD_STATE, SSD_D_INNER), F32)],
        compiler_params=_params(("parallel", "arbitrary")),
        name="ssd_bwd",
    )(y1, c16, sb, dt_raw, zs, dtb, alog, rb, ng)


NA_KEYS = WIN_H * GRID_W


NA_ROWS_PER_STEP = 32


def _na_body(q_ref, k_ref, v_ref, bias_ref, o_ref, *, rows):
    scale = NA_HEAD_DIM ** -0.5

    def step(i, carry):
        qss, kss, scores, probs = [], [], [], []
        for j in range(NA_ROWS_PER_STEP):
            r = i * NA_ROWS_PER_STEP + j
            rs = jnp.clip(r - WIN_H // 2, 0, rows - WIN_H)
            dy0 = rs - r + (WIN_H - 1)
            qs = pl.ds(pl.multiple_of(r * GRID_W, GRID_W), GRID_W)
            ks = pl.ds(pl.multiple_of(rs * GRID_W, GRID_W), NA_KEYS)
            s = lax.dot_general(q_ref[0, qs, :], k_ref[0, ks, :], (((1,), (1,)), ((), ())),
                                preferred_element_type=F32)
            scores.append(s * scale + bias_ref[0, dy0])
            qss.append(qs)
            kss.append(ks)
        for s in scores:
            p = jnp.exp(s - jnp.max(s, axis=-1, keepdims=True))
            probs.append((p.astype(BF16), 1.0 / jnp.sum(p, axis=-1, keepdims=True)))
        for qs, ks, (p, inv) in zip(qss, kss, probs):
            o = jnp.dot(p, v_ref[0, ks, :], preferred_element_type=F32)
            o_ref[0, qs, :] = (o * inv).astype(o_ref.dtype)
        return carry

    lax.fori_loop(0, rows // NA_ROWS_PER_STEP, step, 0)


def _na_bias_table(rpb):
    qc = np.arange(GRID_W)[:, None]
    kc = np.arange(GRID_W)[None, :]
    wstart = np.clip(qc - WIN_W // 2, 0, GRID_W - WIN_W)
    ok = (kc >= wstart) & (kc < wstart + WIN_W)
    dx = np.clip(kc - qc + WIN_W - 1, 0, 2 * WIN_W - 2)
    per_dy = jnp.where(ok[None, None], rpb[:, :, dx].astype(F32), -jnp.inf)
    dy = np.arange(WIN_H)[:, None] + np.arange(WIN_H)[None, :]
    tab = per_dy[:, dy]
    return jnp.transpose(tab, (0, 1, 3, 2, 4)).reshape(NA_HEADS, WIN_H, GRID_W, NA_KEYS)


def _natten(qkv, bias_tab):
    bsz, seq, _ = qkv.shape
    rows = seq // GRID_W
    assert seq % GRID_W == 0 and rows >= WIN_H and rows % NA_ROWS_PER_STEP == 0
    blk = lambda off: pl.BlockSpec((1, seq, NA_HEAD_DIM), lambda b, h: (b, 0, off + h))
    return pl.pallas_call(
        functools.partial(_na_body, rows=rows),
        grid=(bsz, NA_HEADS),
        in_specs=[blk(0), blk(NA_HEADS), blk(2 * NA_HEADS),
                  pl.BlockSpec((1, WIN_H, GRID_W, NA_KEYS), lambda b, h: (h, 0, 0, 0))],
        out_specs=blk(0),
        out_shape=jax.ShapeDtypeStruct((bsz, seq, D_MODEL), BF16),
        compiler_params=_params(("parallel", "parallel")),
        name="natten",
    )(qkv, qkv, qkv, bias_tab)


def _row(v):
    return v.reshape(1, -1).astype(F32)


def _rg_layer(x, shape, p):
    bsz, seq = shape
    gate = _matmul(x, p["w_gate"], act="gelu", out_dtype=BF16, bn=2048, name="rg_in_gate")
    uc = _matmul_conv(x.reshape(bsz, seq, D_MODEL), p["w_u"], p["cw"], p["cb"], name="rg_in_u_conv")
    y = _rg_core(uc, gate.reshape(bsz, seq, D_MODEL), p)
    return y.reshape(bsz * seq, D_MODEL), p["w_out"]


def _ssd_layer(x, shape, p):
    bsz, seq = shape
    zs = _matmul(x, p["w_z"], act="silu", out_dtype=BF16, bn=2048, name="ssd_in_z").reshape(bsz, seq, SSD_D_INNER)
    xbc = _matmul_conv(x.reshape(bsz, seq, D_MODEL), p["w_xbc"], p["cw"], p["cb"], act="silu", name="ssd_in_xbc_conv")
    dt = _matmul(x, p["w_dt"], bn=2 * SSD_HEADS, name="ssd_in_dt").reshape(bsz, seq, 2 * SSD_HEADS)
    y = _ssd_core(xbc, dt, zs, p["dtb"], p["alog"], p["dskip_x"], p["rf"], p["rb"], p["ng"])
    return y.reshape(bsz * seq, SSD_D_INNER), p["w_out"]


def _na_layer(x, shape, p):
    bsz, seq = shape
    qkv = _matmul(x, p["w_qkv"], p["b_qkv"], out_dtype=BF16, bn=2048, name="na_qkv").reshape(bsz, seq, 3 * D_MODEL)
    o = _natten(qkv, p["bias_tab"])
    return o.reshape(bsz * seq, D_MODEL), p["w_out"]


def _head_expand_matrix(offset):
    r = np.zeros((2 * SSD_HEADS, SSD_D_INNER), np.float32)
    for h in range(SSD_HEADS):
        r[offset + h, h * SSD_HEAD_DIM:(h + 1) * SSD_HEAD_DIM] = 1.0
    return jnp.asarray(r, BF16)


def _rg_params(w_in, conv_w, conv_b, w_a, b_a, w_x, b_x, lam, w_out):
    return dict(w_gate=w_in[:, :D_MODEL].astype(BF16), w_u=w_in[:, D_MODEL:].astype(BF16),
                cw=conv_w, cb=_row(conv_b),
                wg=jnp.concatenate([w_a, w_x], axis=-1).astype(BF16),
                ba=b_a.reshape(2, 1, D_MODEL), bx=b_x.reshape(2, 1, D_MODEL), lam=lam.reshape(2, 1, D_MODEL),
                w_out=w_out.astype(BF16))


def _ssd_params(w_in, conv_w, conv_b, dt_bias, a_log, d_skip, norm_g, w_out):
    return dict(w_z=w_in[:, :SSD_D_INNER].astype(BF16),
                w_xbc=w_in[:, SSD_D_INNER:SSD_D_INNER + SSD_CONV_DIM].astype(BF16),
                w_dt=w_in[:, SSD_D_INNER + SSD_CONV_DIM:].astype(BF16),
                cw=conv_w, cb=_row(conv_b), dtb=_row(dt_bias), alog=_row(a_log),
                dskip_x=_row(jnp.repeat(d_skip, SSD_HEAD_DIM)),
                rf=_head_expand_matrix(0), rb=_head_expand_matrix(SSD_HEADS),
                ng=_row(norm_g), w_out=w_out.astype(BF16))


def _na_params(w_qkv, b_qkv, rpb, w_out):
    return dict(w_qkv=w_qkv.astype(BF16), b_qkv=_row(b_qkv), bias_tab=_na_bias_table(rpb),
                w_out=w_out.astype(BF16))


def kernel(x_prompt, x_sample, rg_w_in, rg_conv_w, rg_conv_b, rg_w_a, rg_b_a, rg_w_x, rg_b_x, rg_lambda, rg_w_out,
           ssd_w_in, ssd_conv_w, ssd_conv_b, ssd_dt_bias, ssd_a_log, ssd_d, ssd_norm_g, ssd_w_out,
           na_w_qkv, na_b_qkv, na_rpb, na_w_out, mlp_w_up, mlp_w_down, ln1_g, ln1_b, ln2_g, ln2_b):
    layers = []
    for i in range(DEPTH):
        kind, j = i % 3, i // 3
        if kind == 0:
            fn, p = _rg_layer, _rg_params(rg_w_in[j], rg_conv_w[j], rg_conv_b[j], rg_w_a[j], rg_b_a[j], rg_w_x[j],
                                          rg_b_x[j], rg_lambda[j], rg_w_out[j])
        elif kind == 1:
            fn, p = _ssd_layer, _ssd_params(ssd_w_in[j], ssd_conv_w[j], ssd_conv_b[j], ssd_dt_bias[j], ssd_a_log[j],
                                            ssd_d[j], ssd_norm_g[j], ssd_w_out[j])
        else:
            fn, p = _na_layer, _na_params(na_w_qkv[j], na_b_qkv[j], na_rpb[j], na_w_out[j])
        layers.append((fn, p, mlp_w_up[i].astype(BF16), mlp_w_down[i].astype(BF16),
                       _row(ln1_g[i]), _row(ln1_b[i]), _row(ln2_g[i]), _row(ln2_b[i])))

    def trunk(x):
        bsz, seq, _ = x.shape
        x = x.reshape(bsz * seq, D_MODEL)
        for fn, p, w_up, w_down, g1, b1, g2, b2 in layers:
            y, w_out = fn(x, (bsz, seq), p)
            x = _matmul_ln(y, w_out, x, g1, b1, name="mixer_out_ln")
            hid = _matmul(x, w_up, act="relu2", out_dtype=BF16, bn=2048, name="mlp_up")
            x = _matmul_ln(hid, w_down, x, g2, b2, name="mlp_down_ln")
        return x.reshape(bsz, seq, D_MODEL)

    return trunk(x_prompt), trunk(x_sample)
```

```python
import functools
import math
from typing import NamedTuple

import numpy as np
import jax
import jax.numpy as jnp
from jax import lax
from jax.experimental import pallas as pl
from jax.experimental.pallas import tpu as pltpu

F32 = jnp.float32
BF16 = jnp.bfloat16

D_MODEL = 2048
DEPTH = 4
ALPHA = (2 * DEPTH) ** 0.25
LN_EPS = 1e-5
CONV_W = 4
LRU_BLOCKS = 8
LRU_BW = D_MODEL // LRU_BLOCKS
LRU_C = 8.0
SSD_D_INNER = 2 * D_MODEL
SSD_HEAD_DIM = 64
SSD_HEADS = SSD_D_INNER // SSD_HEAD_DIM
SSD_GROUPS = 8
SSD_STATE = 128
SSD_CHUNK = 128
SSD_GROUP_W = SSD_D_INNER // SSD_GROUPS
SSD_CONV_DIM = SSD_D_INNER + 2 * SSD_GROUPS * SSD_STATE
GRID_W = 64
WIN_H = 8
WIN_W = 16
NA_HEAD_DIM = 128
NA_HEADS = D_MODEL // NA_HEAD_DIM
MLP_HIDDEN = 4 * D_MODEL

V7X_SUBLANES = 8
V7X_LANES = 128
V7X_VMEM_LIMIT_BYTES = 56 * 1024 * 1024


class _Weight(NamedTuple):
    stack: jax.Array
    layer: int
    col_off: int
    n: int

    def spec(self, rows, bn, index):
        assert self.col_off % bn == 0 and self.n % bn == 0
        off = self.col_off // bn
        return pl.BlockSpec((None, rows, bn), lambda *g: (self.layer, index(*g)[0], off + index(*g)[1]))


def _params(semantics, vmem_bytes=V7X_VMEM_LIMIT_BYTES):
    return pltpu.CompilerParams(dimension_semantics=semantics, vmem_limit_bytes=vmem_bytes)


def _sigmoid(x):
    return 1.0 / (1.0 + jnp.exp(-x))


def _softplus(x):
    return jnp.maximum(x, 0.0) + jnp.log1p(jnp.exp(-jnp.abs(x)))


def _gelu_tanh(x):
    c = math.sqrt(2.0 / math.pi)
    return 0.5 * x * (1.0 + jnp.tanh(c * (x + 0.044715 * (x * x * x))))


def _relu2(x):
    r = jnp.maximum(x, 0.0)
    return r * r


def _silu(x):
    return x * _sigmoid(x)


_ACTS = {"none": lambda v: v, "gelu": _gelu_tanh, "relu2": _relu2, "silu": _silu}


def _mm_body(*refs, act, has_bias):
    if has_bias:
        x_ref, w_ref, b_ref, o_ref = refs
    else:
        x_ref, w_ref, o_ref = refs
    acc = jnp.dot(x_ref[...].astype(BF16), w_ref[...], preferred_element_type=F32)
    if has_bias:
        acc = acc + b_ref[...]
    o_ref[...] = _ACTS[act](acc).astype(o_ref.dtype)


def _matmul(x, w, bias=None, *, act="none", out_dtype=F32, bm=1024, bn=1024, name="mm"):
    m, k = x.shape
    n = w.n
    assert k == w.stack.shape[1] and m % bm == 0, (x.shape, w.stack.shape, bm, bn)
    in_specs = [pl.BlockSpec((bm, k), lambda i, j: (i, 0)),
                w.spec(k, bn, lambda i, j: (0, j))]
    args = [x, w.stack]
    if bias is not None:
        in_specs.append(pl.BlockSpec((1, bn), lambda i, j: (0, j)))
        args.append(bias)
    return pl.pallas_call(
        functools.partial(_mm_body, act=act, has_bias=bias is not None),
        grid=(m // bm, n // bn),
        in_specs=in_specs,
        out_specs=pl.BlockSpec((bm, bn), lambda i, j: (i, j)),
        out_shape=jax.ShapeDtypeStruct((m, n), out_dtype),
        compiler_params=_params(("parallel", "parallel")),
        name=name,
    )(*args)


CONV_HALO = 16
CONV_SUBDOT = 256


def _mm_conv_body(x_ref, xp_ref, xn_ref, w_ref, cw_ref, cb_ref, o_ref, *, act, n_tiles):
    t = pl.program_id(1)
    rows = x_ref.shape[1]
    n = rows + 2 * CONV_HALO
    lo, hi = CONV_HALO, CONV_HALO + rows
    ext = jnp.concatenate([xp_ref[0] * (t > 0).astype(xp_ref.dtype), x_ref[0],
                           xn_ref[0] * (t < n_tiles - 1).astype(xn_ref.dtype)], axis=0).astype(BF16)
    for c in range(o_ref.shape[2] // CONV_SUBDOT):
        cols = slice(c * CONV_SUBDOT, (c + 1) * CONV_SUBDOT)
        y = jnp.dot(ext, w_ref[:, cols], preferred_element_type=F32)
        v = (cb_ref[:, cols] + pltpu.roll(y, 1, 0)[lo:hi] * cw_ref[0:1, cols] + y[lo:hi] * cw_ref[1:2, cols]
             + pltpu.roll(y, n - 1, 0)[lo:hi] * cw_ref[2:3, cols] + pltpu.roll(y, n - 2, 0)[lo:hi] * cw_ref[3:4, cols])
        o_ref[0, :, cols] = _ACTS[act](v).astype(o_ref.dtype)


def _matmul_conv(x, w, cw, cb, *, act="none", tl=1024, bn=1024, name="mm_conv"):
    bsz, seq, k = x.shape
    n = w.n
    assert seq % tl == 0 and bn % CONV_SUBDOT == 0 and tl % CONV_HALO == 0 and k == w.stack.shape[1]
    n_tiles = seq // tl
    hb = tl // CONV_HALO
    n_hblk = seq // CONV_HALO
    return pl.pallas_call(
        functools.partial(_mm_conv_body, act=act, n_tiles=n_tiles),
        grid=(bsz, n_tiles, n // bn),
        in_specs=[pl.BlockSpec((1, tl, k), lambda b, t, j: (b, t, 0)),
                  pl.BlockSpec((1, CONV_HALO, k), lambda b, t, j: (b, jnp.maximum(t * hb - 1, 0), 0)),
                  pl.BlockSpec((1, CONV_HALO, k), lambda b, t, j: (b, jnp.minimum((t + 1) * hb, n_hblk - 1), 0)),
                  w.spec(k, bn, lambda b, t, j: (0, j)),
                  pl.BlockSpec((CONV_W, bn), lambda b, t, j: (0, j)),
                  pl.BlockSpec((1, bn), lambda b, t, j: (0, j))],
        out_specs=pl.BlockSpec((1, tl, bn), lambda b, t, j: (b, t, j)),
        out_shape=jax.ShapeDtypeStruct((bsz, seq, n), F32),
        compiler_params=_params(("parallel", "parallel", "parallel")),
        name=name,
    )(x, x, x, w.stack, cw, cb)


LN_ROW_SPLIT = 4


def _mm_ln_body(x_ref, w_ref, res_ref, g_ref, b_ref, o_ref, *, nk):
    def dot(rows):
        return jnp.dot(x_ref[rows, :], w_ref[...], preferred_element_type=F32)

    def finish():
        hm = x_ref.shape[0] // LN_ROW_SPLIT
        for h in range(LN_ROW_SPLIT):
            rows = slice(h * hm, (h + 1) * hm)
            z = ALPHA * res_ref[rows, :] + dot(rows)
            if nk > 1:
                z = z + o_ref[rows, :]
            mu = jnp.mean(z, axis=-1, keepdims=True)
            zc = z - mu
            var = jnp.mean(zc * zc, axis=-1, keepdims=True)
            o_ref[rows, :] = zc * lax.rsqrt(var + LN_EPS) * g_ref[...] + b_ref[...]

    if nk == 1:
        finish()
        return
    kk = pl.program_id(1)
    every = slice(None)

    @pl.when(kk == 0)
    def _():
        o_ref[...] = dot(every)

    if nk > 2:
        @pl.when(jnp.logical_and(kk > 0, kk < nk - 1))
        def _():
            o_ref[...] += dot(every)

    pl.when(kk == nk - 1)(finish)


def _matmul_ln(x, w, res, g, b, *, bm=1024, bk=1024, name="mm_ln"):
    m, k = x.shape
    n = w.n
    assert k == w.stack.shape[1] and n == D_MODEL and m % bm == 0 and k % bk == 0 and bm % LN_ROW_SPLIT == 0
    nk = k // bk
    return pl.pallas_call(
        functools.partial(_mm_ln_body, nk=nk),
        grid=(m // bm, nk),
        in_specs=[pl.BlockSpec((bm, bk), lambda i, kk: (i, kk)),
                  w.spec(bk, n, lambda i, kk: (kk, 0)),
                  pl.BlockSpec((bm, n), lambda i, kk: (i, 0)),
                  pl.BlockSpec((1, n), lambda i, kk: (0, 0)),
                  pl.BlockSpec((1, n), lambda i, kk: (0, 0))],
        out_specs=pl.BlockSpec((bm, n), lambda i, kk: (i, 0)),
        out_shape=jax.ShapeDtypeStruct((m, n), F32),
        compiler_params=_params(("parallel", "arbitrary")),
        name=name,
    )(x, w.stack, res, g, b)


RG_TILE = 512
RG_CHUNKS = V7X_SUBLANES
RG_CHUNK_ROWS = RG_TILE // RG_CHUNKS
RG_SLABS = D_MODEL // V7X_LANES
LOG2E = math.log2(math.e)
SQRT_FLOOR = 1e-30


def _sigmoid_tanh(x):
    return 0.5 * jnp.tanh(0.5 * x) + 0.5


def _slab_rows(slab, chunk):
    return pl.ds(slab * RG_TILE + chunk, RG_CHUNK_ROWS, stride=RG_CHUNKS)


def _rg_gates(uc, wg_ref, ba_ref, bx_ref, lam_ref, a_s, b_s):
    nsp2 = (-LRU_C * LOG2E) * _softplus(-lam_ref[...])
    for n in range(LRU_BLOCKS):
        sl = slice(n * LRU_BW, (n + 1) * LRU_BW)
        un = uc[:, sl]
        gates = jnp.dot(un.astype(BF16), wg_ref[n], preferred_element_type=F32)
        r = _sigmoid_tanh(gates[:, :LRU_BW] + ba_ref[:, sl])
        i = _sigmoid_tanh(gates[:, LRU_BW:] + bx_ref[:, sl])
        a = jnp.exp2(r * nsp2[:, sl])
        v = 1.0 - a * a
        b = (v * lax.rsqrt(jnp.maximum(v, SQRT_FLOOR))) * (i * un)
        for half in range(LRU_BW // V7X_LANES):
            slab = n * (LRU_BW // V7X_LANES) + half
            lanes = slice(half * V7X_LANES, (half + 1) * V7X_LANES)
            for c in range(RG_CHUNKS):
                rows = slice(c * RG_CHUNK_ROWS, (c + 1) * RG_CHUNK_ROWS)
                a_s[_slab_rows(slab, c), :] = a[rows, lanes]
                b_s[_slab_rows(slab, c), :] = b[rows, lanes]


def _rg_scan_tile(a_s, b_s, carry, *, reverse):
    group = RG_SLABS * V7X_SUBLANES

    def step(i, hp):
        h, p = hp
        t = (RG_CHUNK_ROWS - 1 - i) if reverse else i
        rows = [pl.ds(pl.multiple_of(slab * RG_TILE + t * V7X_SUBLANES, V7X_SUBLANES), V7X_SUBLANES)
                for slab in range(RG_SLABS)]
        a = jnp.concatenate([a_s[r, :] for r in rows], axis=0)
        b = jnp.concatenate([b_s[r, :] for r in rows], axis=0)
        h = a * h + b
        p = p * a
        for slab, r in enumerate(rows):
            a_s[r, :] = h[slab * V7X_SUBLANES:(slab + 1) * V7X_SUBLANES]
            b_s[r, :] = p[slab * V7X_SUBLANES:(slab + 1) * V7X_SUBLANES]
        return h, p

    zeros = jnp.zeros((group, V7X_LANES), F32)
    q, p = lax.fori_loop(0, RG_CHUNK_ROWS, step, (zeros, zeros + 1.0))

    sub = lax.broadcasted_iota(jnp.int32, (group, V7X_LANES), 0) % V7X_SUBLANES
    for s in (1, 2, 4):
        shift = (group - s) if reverse else s
        m = (sub < V7X_SUBLANES - s) if reverse else (sub >= s)
        q = jnp.where(m, p * pltpu.roll(q, shift, 0) + q, q)
        p = jnp.where(m, p * pltpu.roll(p, shift, 0), p)
    c_in = carry[...]
    after = p * c_in + q
    first = (sub == V7X_SUBLANES - 1) if reverse else (sub == 0)
    entering = jnp.where(first, c_in, pltpu.roll(after, (group - 1) if reverse else 1, 0))
    last = 0 if reverse else V7X_SUBLANES - 1
    carry[...] = jnp.concatenate(
        [jnp.broadcast_to(after[slab * V7X_SUBLANES + last:slab * V7X_SUBLANES + last + 1, :],
                          (V7X_SUBLANES, V7X_LANES)) for slab in range(RG_SLABS)], axis=0)
    return entering


def _rg_body(*refs, reverse):
    if reverse:
        uc_ref, wg_ref, ba_ref, bx_ref, lam_ref, hf_ref, gate_ref, o_ref, a_s, b_s, carry = refs
    else:
        uc_ref, wg_ref, ba_ref, bx_ref, lam_ref, o_ref, a_s, b_s, carry = refs

    @pl.when(pl.program_id(1) == 0)
    def _():
        carry[...] = jnp.zeros_like(carry)

    _rg_gates(uc_ref[0], wg_ref, ba_ref, bx_ref, lam_ref, a_s, b_s)
    entering = _rg_scan_tile(a_s, b_s, carry, reverse=reverse)
    for slab in range(RG_SLABS):
        lanes = slice(slab * V7X_LANES, (slab + 1) * V7X_LANES)
        for c in range(RG_CHUNKS):
            rows = slice(c * RG_CHUNK_ROWS, (c + 1) * RG_CHUNK_ROWS)
            e = entering[slab * V7X_SUBLANES + c:slab * V7X_SUBLANES + c + 1, :]
            h = a_s[_slab_rows(slab, c), :] + b_s[_slab_rows(slab, c), :] * e
            if reverse:
                o_ref[0, rows, lanes] = ((hf_ref[0, rows, lanes] + h)
                                         * gate_ref[0, rows, lanes].astype(F32)).astype(o_ref.dtype)
            else:
                o_ref[0, rows, lanes] = h


def _rg_core(uc, gate, p):
    bsz, seq, d = uc.shape
    tl = RG_TILE
    assert seq % tl == 0 and d == D_MODEL
    n_tiles = seq // tl
    row = pl.BlockSpec((1, d), lambda b, t: (0, 0))
    gate_w = pl.BlockSpec((LRU_BLOCKS, LRU_BW, 2 * LRU_BW), lambda b, t: (0, 0, 0))
    scan_scratch = [pltpu.VMEM((RG_SLABS * tl, V7X_LANES), F32), pltpu.VMEM((RG_SLABS * tl, V7X_LANES), F32),
                    pltpu.VMEM((RG_SLABS * V7X_SUBLANES, V7X_LANES), F32)]

    fwd = pl.BlockSpec((1, tl, d), lambda b, t: (b, t, 0))
    hf = pl.pallas_call(
        functools.partial(_rg_body, reverse=False),
        grid=(bsz, n_tiles),
        in_specs=[fwd, gate_w, row, row, row],
        out_specs=fwd,
        out_shape=jax.ShapeDtypeStruct((bsz, seq, d), F32),
        scratch_shapes=scan_scratch,
        compiler_params=_params(("parallel", "arbitrary")),
        name="rg_fwd",
    )(uc, p["wg"][0], p["ba"][0], p["bx"][0], p["lam"][0])

    bwd = pl.BlockSpec((1, tl, d), lambda b, t: (b, n_tiles - 1 - t, 0))
    return pl.pallas_call(
        functools.partial(_rg_body, reverse=True),
        grid=(bsz, n_tiles),
        in_specs=[bwd, gate_w, row, row, row, bwd, bwd],
        out_specs=bwd,
        out_shape=jax.ShapeDtypeStruct((bsz, seq, d), BF16),
        scratch_shapes=scan_scratch,
        compiler_params=_params(("parallel", "arbitrary")),
        name="rg_bwd",
    )(uc, p["wg"][1], p["ba"][1], p["bx"][1], p["lam"][1], hf, gate)


def _prefix_rows(x, rid):
    q = x.shape[0]
    s = 1
    while s < q:
        x = x + jnp.where(rid >= s, pltpu.roll(x, s, 0), 0.0)
        s *= 2
    return x


def _suffix_rows(x, rid):
    q = x.shape[0]
    s = 1
    while s < q:
        x = x + jnp.where(rid < q - s, pltpu.roll(x, q - s, 0), 0.0)
        s *= 2
    return x


def _expand_heads(v, r_ref, *, split=True):
    hi = v.astype(BF16)
    out = jnp.dot(hi, r_ref[...], preferred_element_type=F32)
    if split:
        out = out + jnp.dot((v - hi.astype(F32)).astype(BF16), r_ref[...], preferred_element_type=F32)
    return out


def _ssd_dt(dt_ref, dtb_ref, alog_ref):
    dt = _softplus(dt_ref[0] + dtb_ref[...])
    return dt, dt * (-jnp.exp(alog_ref[...]))


def _ssd1_body(x_ref, dt_ref, dtb_ref, alog_ref, dsk_ref, rf_ref, rb_ref, y1_ref, c_ref, sb_ref, hf_s):
    q = SSD_CHUNK

    @pl.when(pl.program_id(1) == 0)
    def _():
        hf_s[...] = jnp.zeros_like(hf_s)

    act = x_ref[0]
    xs = act[:, :SSD_D_INNER]
    bm = act[:, SSD_D_INNER:SSD_D_INNER + SSD_GROUPS * SSD_STATE]
    cm = act[:, SSD_D_INNER + SSD_GROUPS * SSD_STATE:]
    c_ref[0] = cm.astype(BF16)

    dt, dta = _ssd_dt(dt_ref, dtb_ref, alog_ref)
    rid = lax.broadcasted_iota(jnp.int32, (q, q), 0)
    cid = lax.broadcasted_iota(jnp.int32, (q, q), 1)
    is_fwd = cid < SSD_HEADS
    cum = jnp.where(is_fwd, _prefix_rows(dta, rid), _suffix_rows(dta, rid))
    tot = jnp.where(is_fwd[0:1, :], cum[q - 1:q, :], cum[0:1, :])
    e_in = jnp.exp(cum)
    w_end = dt * jnp.exp(tot - cum)

    ef_x = _expand_heads(e_in, rf_ref)
    xf = (xs * _expand_heads(w_end, rf_ref, split=False)).astype(BF16)
    xb = (xs * _expand_heads(w_end, rb_ref, split=False)).astype(BF16)

    dst = cum * LOG2E
    src_t = (dst - jnp.log2(dt)).T
    lower = rid >= cid
    upper = rid <= cid
    neg_inf = jnp.float32(-jnp.inf)
    lane_lo = cid < SSD_HEAD_DIM

    for g in range(SSD_GROUPS):
        ns = slice(g * SSD_STATE, (g + 1) * SSD_STATE)
        gs = slice(g * SSD_GROUP_W, (g + 1) * SSD_GROUP_W)
        bg = bm[:, ns]
        cg16 = cm[:, ns].astype(BF16)
        cb = lax.dot_general(cg16, bg.astype(BF16), (((1,), (1,)), ((), ())), preferred_element_type=F32)
        bgt16 = bg.T.astype(BF16)
        sf_t = jnp.dot(bgt16, xf[:, gs], preferred_element_type=F32)
        sb_ref[0, 0, :, gs] = jnp.dot(bgt16, xb[:, gs], preferred_element_type=F32).astype(sb_ref.dtype)
        h_in = hf_s[:, gs]
        y_off = jnp.dot(cg16, h_in.astype(BF16), preferred_element_type=F32) * ef_x[:, gs]
        hf_s[:, gs] = h_in * ef_x[q - 1:q, gs] + sf_t

        for pr in range(SSD_GROUP_W // (2 * SSD_HEAD_DIM)):
            h0 = g * (SSD_HEADS // SSD_GROUPS) + 2 * pr
            ms = []
            for h in (h0, h0 + 1):
                hb = SSD_HEADS + h
                d_f = dst[:, h:h + 1] - src_t[h:h + 1, :]
                d_b = dst[:, hb:hb + 1] - src_t[hb:hb + 1, :]
                m_f = jnp.exp2(jnp.where(lower, d_f, neg_inf))
                m_b = jnp.exp2(jnp.where(upper, d_b, neg_inf))
                ms.append((cb * (m_f + m_b)).astype(BF16))
            lhs = jnp.concatenate(ms, axis=1)
            ps = slice(h0 * SSD_HEAD_DIM, (h0 + 2) * SSD_HEAD_DIM)
            xp = xs[:, ps]
            rhs = jnp.concatenate([jnp.where(lane_lo, xp, 0.0).astype(BF16),
                                   jnp.where(lane_lo, 0.0, xp).astype(BF16)], axis=0)
            y_diag = jnp.dot(lhs, rhs, preferred_element_type=F32)
            os_ = slice(pr * 2 * SSD_HEAD_DIM, (pr + 1) * 2 * SSD_HEAD_DIM)
            y1_ref[0, :, ps] = (y_diag + y_off[:, os_] + xs[:, ps] * dsk_ref[:, ps]).astype(y1_ref.dtype)


def _ssd2_body(y1_ref, c_ref, sb_ref, dt_ref, zs_ref, dtb_ref, alog_ref, rb_ref, ng_ref, o_ref, hb_s):
    q = SSD_CHUNK
    c = pl.program_id(1)

    @pl.when(c == 0)
    def _():
        hb_s[...] = jnp.zeros_like(hb_s)

    _, dta = _ssd_dt(dt_ref, dtb_ref, alog_ref)
    rid = lax.broadcasted_iota(jnp.int32, (q, q), 0)
    eb_x = _expand_heads(jnp.exp(_suffix_rows(dta, rid)), rb_ref)

    for g in range(SSD_GROUPS):
        ns = slice(g * SSD_STATE, (g + 1) * SSD_STATE)
        gs = slice(g * SSD_GROUP_W, (g + 1) * SSD_GROUP_W)
        h_in = hb_s[:, gs]
        y_off = jnp.dot(c_ref[0, :, ns], h_in.astype(BF16), preferred_element_type=F32) * eb_x[:, gs]
        hb_s[:, gs] = h_in * eb_x[0:1, gs] + sb_ref[0, 0, :, gs].astype(F32)
        y = (y1_ref[0, :, gs].astype(F32) + y_off) * zs_ref[0, :, gs].astype(F32)
        ms = jnp.mean(y * y, axis=-1, keepdims=True)
        o_ref[0, :, gs] = (y * lax.rsqrt(ms + LN_EPS) * ng_ref[:, gs]).astype(BF16)


def _ssd_core(xbc_act, dt_raw, zs, dtb, alog, dskip_x, rf, rb, ng):
    bsz, seq, _ = xbc_act.shape
    q = SSD_CHUNK
    assert seq % q == 0 and q == 2 * SSD_HEADS
    nc = seq // q
    row = lambda r, w: pl.BlockSpec((r, w), lambda b, c: (0, 0))
    y1, c16, sb = pl.pallas_call(
        _ssd1_body,
        grid=(bsz, nc),
        in_specs=[
            pl.BlockSpec((1, q, SSD_CONV_DIM), lambda b, c: (b, c, 0)),
            pl.BlockSpec((1, q, 2 * SSD_HEADS), lambda b, c: (b, c, 0)),
            row(1, 2 * SSD_HEADS), row(1, 2 * SSD_HEADS),
            row(1, SSD_D_INNER), row(2 * SSD_HEADS, SSD_D_INNER), row(2 * SSD_HEADS, SSD_D_INNER),
        ],
        out_specs=[
            pl.BlockSpec((1, q, SSD_D_INNER), lambda b, c: (b, c, 0)),
            pl.BlockSpec((1, q, SSD_GROUPS * SSD_STATE), lambda b, c: (b, c, 0)),
            pl.BlockSpec((1, 1, SSD_STATE, SSD_D_INNER), lambda b, c: (b, c, 0, 0)),
        ],
        out_shape=[
            jax.ShapeDtypeStruct((bsz, seq, SSD_D_INNER), BF16),
            jax.ShapeDtypeStruct((bsz, seq, SSD_GROUPS * SSD_STATE), BF16),
            jax.ShapeDtypeStruct((bsz, nc, SSD_STATE, SSD_D_INNER), BF16),
        ],
        scratch_shapes=[pltpu.VMEM((SSD_STATE, SSD_D_INNER), F32)],
        compiler_params=_params(("parallel", "arbitrary")),
        name="ssd_fwd",
    )(xbc_act, dt_raw, dtb, alog, dskip_x, rf, rb)

    rev = lambda b, c: (b, nc - 1 - c, 0)
    return pl.pallas_call(
        _ssd2_body,
        grid=(bsz, nc),
        in_specs=[
            pl.BlockSpec((1, q, SSD_D_INNER), rev),
            pl.BlockSpec((1, q, SSD_GROUPS * SSD_STATE), rev),
            pl.BlockSpec((1, 1, SSD_STATE, SSD_D_INNER), lambda b, c: (b, nc - 1 - c, 0, 0)),
            pl.BlockSpec((1, q, 2 * SSD_HEADS), rev),
            pl.BlockSpec((1, q, SSD_D_INNER), rev),
            row(1, 2 * SSD_HEADS), row(1, 2 * SSD_HEADS), row(2 * SSD_HEADS, SSD_D_INNER), row(1, SSD_D_INNER),
        ],
        out_specs=pl.BlockSpec((1, q, SSD_D_INNER), rev),
        out_shape=jax.ShapeDtypeStruct((bsz, seq, SSD_D_INNER), BF16),
        scratch_shapes=[pltpu.VMEM((SSD_STATE, SSD_D_INNER), F32)],
        compiler_params=_params(("parallel", "arbitrary")),
        name="ssd_bwd",
    )(y1, c16, sb, dt_raw, zs, dtb, alog, rb, ng)


NA_KEYS = WIN_H * GRID_W


NA_ROWS_PER_STEP = 32


def _na_body(q_ref, k_ref, v_ref, bias_ref, o_ref, *, rows):
    scale = NA_HEAD_DIM ** -0.5

    def step(i, carry):
        qss, kss, scores, probs = [], [], [], []
        for j in range(NA_ROWS_PER_STEP):
            r = i * NA_ROWS_PER_STEP + j
            rs = jnp.clip(r - WIN_H // 2, 0, rows - WIN_H)
            dy0 = rs - r + (WIN_H - 1)
            qs = pl.ds(pl.multiple_of(r * GRID_W, GRID_W), GRID_W)
            ks = pl.ds(pl.multiple_of(rs * GRID_W, GRID_W), NA_KEYS)
            s = lax.dot_general(q_ref[0, qs, :], k_ref[0, ks, :], (((1,), (1,)), ((), ())),
                                preferred_element_type=F32)
            scores.append(s * scale + bias_ref[0, dy0])
            qss.append(qs)
            kss.append(ks)
        for s in scores:
            p = jnp.exp(s - jnp.max(s, axis=-1, keepdims=True))
            probs.append((p.astype(BF16), 1.0 / jnp.sum(p, axis=-1, keepdims=True)))
        for qs, ks, (p, inv) in zip(qss, kss, probs):
            o = jnp.dot(p, v_ref[0, ks, :], preferred_element_type=F32)
            o_ref[0, qs, :] = (o * inv).astype(o_ref.dtype)
        return carry

    lax.fori_loop(0, rows // NA_ROWS_PER_STEP, step, 0)


def _na_bias_table(rpb):
    qc = np.arange(GRID_W)[:, None]
    kc = np.arange(GRID_W)[None, :]
    wstart = np.clip(qc - WIN_W // 2, 0, GRID_W - WIN_W)
    ok = (kc >= wstart) & (kc < wstart + WIN_W)
    dx = np.clip(kc - qc + WIN_W - 1, 0, 2 * WIN_W - 2)
    per_dy = jnp.where(ok[None, None], rpb[:, :, dx].astype(F32), -jnp.inf)
    dy = np.arange(WIN_H)[:, None] + np.arange(WIN_H)[None, :]
    tab = per_dy[:, dy]
    return jnp.transpose(tab, (0, 1, 3, 2, 4)).reshape(NA_HEADS, WIN_H, GRID_W, NA_KEYS)


def _natten(qkv, bias_tab):
    bsz, seq, _ = qkv.shape
    rows = seq // GRID_W
    assert seq % GRID_W == 0 and rows >= WIN_H and rows % NA_ROWS_PER_STEP == 0
    blk = lambda off: pl.BlockSpec((1, seq, NA_HEAD_DIM), lambda b, h: (b, 0, off + h))
    return pl.pallas_call(
        functools.partial(_na_body, rows=rows),
        grid=(bsz, NA_HEADS),
        in_specs=[blk(0), blk(NA_HEADS), blk(2 * NA_HEADS),
                  pl.BlockSpec((1, WIN_H, GRID_W, NA_KEYS), lambda b, h: (h, 0, 0, 0))],
        out_specs=blk(0),
        out_shape=jax.ShapeDtypeStruct((bsz, seq, D_MODEL), BF16),
        compiler_params=_params(("parallel", "parallel")),
        name="natten",
    )(qkv, qkv, qkv, bias_tab)


def _row(v):
    return v.reshape(1, -1).astype(F32)


def _rg_layer(x, shape, p):
    bsz, seq = shape
    gate = _matmul(x, p["w_gate"], act="gelu", out_dtype=BF16, bn=2048, name="rg_in_gate")
    uc = _matmul_conv(x.reshape(bsz, seq, D_MODEL), p["w_u"], p["cw"], p["cb"], name="rg_in_u_conv")
    y = _rg_core(uc, gate.reshape(bsz, seq, D_MODEL), p)
    return y.reshape(bsz * seq, D_MODEL), p["w_out"]


def _ssd_layer(x, shape, p):
    bsz, seq = shape
    zs = _matmul(x, p["w_z"], act="silu", out_dtype=BF16, bn=2048, name="ssd_in_z").reshape(bsz, seq, SSD_D_INNER)
    xbc = _matmul_conv(x.reshape(bsz, seq, D_MODEL), p["w_xbc"], p["cw"], p["cb"], act="silu", name="ssd_in_xbc_conv")
    dt = _matmul(x, p["w_dt"], bn=2 * SSD_HEADS, name="ssd_in_dt").reshape(bsz, seq, 2 * SSD_HEADS)
    y = _ssd_core(xbc, dt, zs, p["dtb"], p["alog"], p["dskip_x"], p["rf"], p["rb"], p["ng"])
    return y.reshape(bsz * seq, SSD_D_INNER), p["w_out"]


def _na_layer(x, shape, p):
    bsz, seq = shape
    qkv = _matmul(x, p["w_qkv"], p["b_qkv"], out_dtype=BF16, bn=2048, name="na_qkv").reshape(bsz, seq, 3 * D_MODEL)
    o = _natten(qkv, p["bias_tab"])
    return o.reshape(bsz * seq, D_MODEL), p["w_out"]


def _head_expand_matrix(offset):
    r = np.zeros((2 * SSD_HEADS, SSD_D_INNER), np.float32)
    for h in range(SSD_HEADS):
        r[offset + h, h * SSD_HEAD_DIM:(h + 1) * SSD_HEAD_DIM] = 1.0
    return jnp.asarray(r, BF16)


def _rg_params(j, w_in16, conv_w, conv_b, w_a, b_a, w_x, b_x, lam, w_out16):
    return dict(w_gate=_Weight(w_in16, j, 0, D_MODEL), w_u=_Weight(w_in16, j, D_MODEL, D_MODEL),
                cw=conv_w[j], cb=_row(conv_b[j]),
                wg=jnp.concatenate([w_a[j], w_x[j]], axis=-1).astype(BF16),
                ba=b_a[j].reshape(2, 1, D_MODEL), bx=b_x[j].reshape(2, 1, D_MODEL),
                lam=lam[j].reshape(2, 1, D_MODEL), w_out=_Weight(w_out16, j, 0, D_MODEL))


def _ssd_params(j, w_in16, conv_w, conv_b, dt_bias, a_log, d_skip, norm_g, w_out16):
    return dict(w_z=_Weight(w_in16, j, 0, SSD_D_INNER),
                w_xbc=_Weight(w_in16, j, SSD_D_INNER, SSD_CONV_DIM),
                w_dt=_Weight(w_in16, j, SSD_D_INNER + SSD_CONV_DIM, 2 * SSD_HEADS),
                cw=conv_w[j], cb=_row(conv_b[j]), dtb=_row(dt_bias[j]), alog=_row(a_log[j]),
                dskip_x=_row(jnp.repeat(d_skip[j], SSD_HEAD_DIM)),
                rf=_head_expand_matrix(0), rb=_head_expand_matrix(SSD_HEADS),
                ng=_row(norm_g[j]), w_out=_Weight(w_out16, j, 0, D_MODEL))


def _na_params(j, w_qkv16, b_qkv, rpb, w_out16):
    return dict(w_qkv=_Weight(w_qkv16, j, 0, 3 * D_MODEL), b_qkv=_row(b_qkv[j]), bias_tab=_na_bias_table(rpb[j]),
                w_out=_Weight(w_out16, j, 0, D_MODEL))


def kernel(x_prompt, x_sample, rg_w_in, rg_conv_w, rg_conv_b, rg_w_a, rg_b_a, rg_w_x, rg_b_x, rg_lambda, rg_w_out,
           ssd_w_in, ssd_conv_w, ssd_conv_b, ssd_dt_bias, ssd_a_log, ssd_d, ssd_norm_g, ssd_w_out,
           na_w_qkv, na_b_qkv, na_rpb, na_w_out, mlp_w_up, mlp_w_down, ln1_g, ln1_b, ln2_g, ln2_b):
    cast = lambda w: w.astype(BF16)
    rg_w_in16, rg_w_out16, ssd_w_in16, ssd_w_out16 = cast(rg_w_in), cast(rg_w_out), cast(ssd_w_in), cast(ssd_w_out)
    na_w_qkv16, na_w_out16, mlp_w_up16, mlp_w_down16 = cast(na_w_qkv), cast(na_w_out), cast(mlp_w_up), cast(mlp_w_down)
    layers = []
    for i in range(DEPTH):
        kind, j = i % 3, i // 3
        if kind == 0:
            fn, p = _rg_layer, _rg_params(j, rg_w_in16, rg_conv_w, rg_conv_b, rg_w_a, rg_b_a, rg_w_x, rg_b_x,
                                          rg_lambda, rg_w_out16)
        elif kind == 1:
            fn, p = _ssd_layer, _ssd_params(j, ssd_w_in16, ssd_conv_w, ssd_conv_b, ssd_dt_bias, ssd_a_log, ssd_d,
                                            ssd_norm_g, ssd_w_out16)
        else:
            fn, p = _na_layer, _na_params(j, na_w_qkv16, na_b_qkv, na_rpb, na_w_out16)
        layers.append((fn, p, _Weight(mlp_w_up16, i, 0, MLP_HIDDEN), _Weight(mlp_w_down16, i, 0, D_MODEL),
                       _row(ln1_g[i]), _row(ln1_b[i]), _row(ln2_g[i]), _row(ln2_b[i])))

    def trunk(x):
        bsz, seq, _ = x.shape
        x = x.reshape(bsz * seq, D_MODEL)
        for fn, p, w_up, w_down, g1, b1, g2, b2 in layers:
            y, w_out = fn(x, (bsz, seq), p)
            x = _matmul_ln(y, w_out, x, g1, b1, name="mixer_out_ln")
            hid = _matmul(x, w_up, act="relu2", out_dtype=BF16, bn=2048, name="mlp_up")
            x = _matmul_ln(hid, w_down, x, g2, b2, name="mlp_down_ln")
        return x.reshape(bsz, seq, D_MODEL)

    return trunk(x_prompt), trunk(x_sample)
```

```python
import functools
import math
from typing import NamedTuple

import numpy as np
import jax
import jax.numpy as jnp
from jax import lax
from jax.experimental import pallas as pl
from jax.experimental.pallas import tpu as pltpu

F32 = jnp.float32
BF16 = jnp.bfloat16

D_MODEL = 2048
DEPTH = 4
ALPHA = (2 * DEPTH) ** 0.25
LN_EPS = 1e-5
CONV_W = 4
LRU_BLOCKS = 8
LRU_BW = D_MODEL // LRU_BLOCKS
LRU_C = 8.0
SSD_D_INNER = 2 * D_MODEL
SSD_HEAD_DIM = 64
SSD_HEADS = SSD_D_INNER // SSD_HEAD_DIM
SSD_GROUPS = 8
SSD_STATE = 128
SSD_CHUNK = 128
SSD_GROUP_W = SSD_D_INNER // SSD_GROUPS
SSD_CONV_DIM = SSD_D_INNER + 2 * SSD_GROUPS * SSD_STATE
GRID_W = 64
WIN_H = 8
WIN_W = 16
NA_HEAD_DIM = 128
NA_HEADS = D_MODEL // NA_HEAD_DIM
MLP_HIDDEN = 4 * D_MODEL

V7X_SUBLANES = 8
V7X_LANES = 128
V7X_VMEM_LIMIT_BYTES = 56 * 1024 * 1024


class _Weight(NamedTuple):
    stack: jax.Array
    layer: int
    col_off: int
    n: int

    def spec(self, rows, bn, index):
        assert self.col_off % bn == 0 and self.n % bn == 0
        off = self.col_off // bn
        return pl.BlockSpec((None, rows, bn), lambda *g: (self.layer, index(*g)[0], off + index(*g)[1]))

    def resident_spec(self):
        k = self.stack.shape[1]
        assert self.col_off % self.n == 0
        return pl.BlockSpec((None, k, self.n), lambda *g: (self.layer, 0, self.col_off // self.n),
                            pipeline_mode=pl.Buffered(1))


def _params(semantics, vmem_bytes=V7X_VMEM_LIMIT_BYTES):
    return pltpu.CompilerParams(dimension_semantics=semantics, vmem_limit_bytes=vmem_bytes)


def _sigmoid(x):
    return 1.0 / (1.0 + jnp.exp(-x))


def _softplus(x):
    return jnp.maximum(x, 0.0) + jnp.log1p(jnp.exp(-jnp.abs(x)))


def _gelu_tanh(x):
    c = math.sqrt(2.0 / math.pi)
    return 0.5 * x * (1.0 + jnp.tanh(c * (x + 0.044715 * (x * x * x))))


def _relu2(x):
    r = jnp.maximum(x, 0.0)
    return r * r


def _silu(x):
    return x * _sigmoid(x)


_ACTS = {"none": lambda v: v, "gelu": _gelu_tanh, "relu2": _relu2, "silu": _silu}


def _mm_body(*refs, act, has_bias):
    if has_bias:
        x_ref, w_ref, b_ref, o_ref = refs
    else:
        x_ref, w_ref, o_ref = refs
    acc = jnp.dot(x_ref[...].astype(BF16), w_ref[...], preferred_element_type=F32)
    if has_bias:
        acc = acc + b_ref[...]
    o_ref[...] = _ACTS[act](acc).astype(o_ref.dtype)


def _matmul(x, w, bias=None, *, act="none", out_dtype=F32, bm=1024, bn=1024, name="mm"):
    m, k = x.shape
    n = w.n
    assert k == w.stack.shape[1] and m % bm == 0, (x.shape, w.stack.shape, bm, bn)
    in_specs = [pl.BlockSpec((bm, k), lambda i, j: (i, 0)),
                w.spec(k, bn, lambda i, j: (0, j))]
    args = [x, w.stack]
    if bias is not None:
        in_specs.append(pl.BlockSpec((1, bn), lambda i, j: (0, j)))
        args.append(bias)
    return pl.pallas_call(
        functools.partial(_mm_body, act=act, has_bias=bias is not None),
        grid=(m // bm, n // bn),
        in_specs=in_specs,
        out_specs=pl.BlockSpec((bm, bn), lambda i, j: (i, j)),
        out_shape=jax.ShapeDtypeStruct((m, n), out_dtype),
        compiler_params=_params(("parallel", "parallel")),
        name=name,
    )(*args)


CONV_HALO = 16
CONV_SUBDOT = 256


def _mm_conv_body(x_ref, xp_ref, xn_ref, w_ref, cw_ref, cb_ref, o_ref, *, act, n_tiles):
    t = pl.program_id(1)
    rows = x_ref.shape[1]
    n = rows + 2 * CONV_HALO
    lo, hi = CONV_HALO, CONV_HALO + rows
    ext = jnp.concatenate([xp_ref[0] * (t > 0).astype(xp_ref.dtype), x_ref[0],
                           xn_ref[0] * (t < n_tiles - 1).astype(xn_ref.dtype)], axis=0).astype(BF16)
    for c in range(o_ref.shape[2] // CONV_SUBDOT):
        cols = slice(c * CONV_SUBDOT, (c + 1) * CONV_SUBDOT)
        y = jnp.dot(ext, w_ref[:, cols], preferred_element_type=F32)
        v = (cb_ref[:, cols] + pltpu.roll(y, 1, 0)[lo:hi] * cw_ref[0:1, cols] + y[lo:hi] * cw_ref[1:2, cols]
             + pltpu.roll(y, n - 1, 0)[lo:hi] * cw_ref[2:3, cols] + pltpu.roll(y, n - 2, 0)[lo:hi] * cw_ref[3:4, cols])
        o_ref[0, :, cols] = _ACTS[act](v).astype(o_ref.dtype)


def _matmul_conv(x, w, cw, cb, *, act="none", tl=1024, bn=1024, name="mm_conv"):
    bsz, seq, k = x.shape
    n = w.n
    assert seq % tl == 0 and bn % CONV_SUBDOT == 0 and tl % CONV_HALO == 0 and k == w.stack.shape[1]
    n_tiles = seq // tl
    hb = tl // CONV_HALO
    n_hblk = seq // CONV_HALO
    return pl.pallas_call(
        functools.partial(_mm_conv_body, act=act, n_tiles=n_tiles),
        grid=(bsz, n_tiles, n // bn),
        in_specs=[pl.BlockSpec((1, tl, k), lambda b, t, j: (b, t, 0)),
                  pl.BlockSpec((1, CONV_HALO, k), lambda b, t, j: (b, jnp.maximum(t * hb - 1, 0), 0)),
                  pl.BlockSpec((1, CONV_HALO, k), lambda b, t, j: (b, jnp.minimum((t + 1) * hb, n_hblk - 1), 0)),
                  w.spec(k, bn, lambda b, t, j: (0, j)),
                  pl.BlockSpec((CONV_W, bn), lambda b, t, j: (0, j)),
                  pl.BlockSpec((1, bn), lambda b, t, j: (0, j))],
        out_specs=pl.BlockSpec((1, tl, bn), lambda b, t, j: (b, t, j)),
        out_shape=jax.ShapeDtypeStruct((bsz, seq, n), F32),
        compiler_params=_params(("parallel", "parallel", "parallel")),
        name=name,
    )(x, x, x, w.stack, cw, cb)


MM_LN_TILES = {D_MODEL: (1024, 4), SSD_D_INNER: (512, 2), MLP_HIDDEN: (256, 2)}


def _mm_ln_body(x_ref, w_ref, res_ref, g_ref, b_ref, o_ref, *, split):
    hm = x_ref.shape[0] // split
    for h in range(split):
        rows = slice(h * hm, (h + 1) * hm)
        z = ALPHA * res_ref[rows, :] + jnp.dot(x_ref[rows, :], w_ref[...], preferred_element_type=F32)
        mu = jnp.mean(z, axis=-1, keepdims=True)
        zc = z - mu
        var = jnp.mean(zc * zc, axis=-1, keepdims=True)
        o_ref[rows, :] = zc * lax.rsqrt(var + LN_EPS) * g_ref[...] + b_ref[...]


def _matmul_ln(x, w, res, g, b, *, name="mm_ln"):
    m, k = x.shape
    n = w.n
    bm, split = MM_LN_TILES[k]
    assert k == w.stack.shape[1] and n == D_MODEL and m % bm == 0 and bm % split == 0
    return pl.pallas_call(
        functools.partial(_mm_ln_body, split=split),
        grid=(m // bm,),
        in_specs=[pl.BlockSpec((bm, k), lambda i: (i, 0)),
                  w.resident_spec(),
                  pl.BlockSpec((bm, n), lambda i: (i, 0)),
                  pl.BlockSpec((1, n), lambda i: (0, 0)),
                  pl.BlockSpec((1, n), lambda i: (0, 0))],
        out_specs=pl.BlockSpec((bm, n), lambda i: (i, 0)),
        out_shape=jax.ShapeDtypeStruct((m, n), F32),
        compiler_params=_params(("parallel",)),
        name=name,
    )(x, w.stack, res, g, b)


RG_TILE = 512
RG_CHUNKS = V7X_SUBLANES
RG_CHUNK_ROWS = RG_TILE // RG_CHUNKS
RG_SLABS = D_MODEL // V7X_LANES
LOG2E = math.log2(math.e)
SQRT_FLOOR = 1e-30


def _sigmoid_tanh(x):
    return 0.5 * jnp.tanh(0.5 * x) + 0.5


def _slab_rows(slab, chunk):
    return pl.ds(slab * RG_TILE + chunk, RG_CHUNK_ROWS, stride=RG_CHUNKS)


def _rg_gates(uc, wg_ref, ba_ref, bx_ref, lam_ref, a_s, b_s):
    nsp2 = (-LRU_C * LOG2E) * _softplus(-lam_ref[...])
    for n in range(LRU_BLOCKS):
        sl = slice(n * LRU_BW, (n + 1) * LRU_BW)
        un = uc[:, sl]
        gates = jnp.dot(un.astype(BF16), wg_ref[n], preferred_element_type=F32)
        r = _sigmoid_tanh(gates[:, :LRU_BW] + ba_ref[:, sl])
        i = _sigmoid_tanh(gates[:, LRU_BW:] + bx_ref[:, sl])
        a = jnp.exp2(r * nsp2[:, sl])
        v = 1.0 - a * a
        b = (v * lax.rsqrt(jnp.maximum(v, SQRT_FLOOR))) * (i * un)
        for half in range(LRU_BW // V7X_LANES):
            slab = n * (LRU_BW // V7X_LANES) + half
            lanes = slice(half * V7X_LANES, (half + 1) * V7X_LANES)
            for c in range(RG_CHUNKS):
                rows = slice(c * RG_CHUNK_ROWS, (c + 1) * RG_CHUNK_ROWS)
                a_s[_slab_rows(slab, c), :] = a[rows, lanes]
                b_s[_slab_rows(slab, c), :] = b[rows, lanes]


def _rg_scan_tile(a_s, b_s, carry, *, reverse):
    group = RG_SLABS * V7X_SUBLANES

    def step(i, hp):
        h, p = hp
        t = (RG_CHUNK_ROWS - 1 - i) if reverse else i
        rows = [pl.ds(pl.multiple_of(slab * RG_TILE + t * V7X_SUBLANES, V7X_SUBLANES), V7X_SUBLANES)
                for slab in range(RG_SLABS)]
        a = jnp.concatenate([a_s[r, :] for r in rows], axis=0)
        b = jnp.concatenate([b_s[r, :] for r in rows], axis=0)
        h = a * h + b
        p = p * a
        for slab, r in enumerate(rows):
            a_s[r, :] = h[slab * V7X_SUBLANES:(slab + 1) * V7X_SUBLANES]
            b_s[r, :] = p[slab * V7X_SUBLANES:(slab + 1) * V7X_SUBLANES]
        return h, p

    zeros = jnp.zeros((group, V7X_LANES), F32)
    q, p = lax.fori_loop(0, RG_CHUNK_ROWS, step, (zeros, zeros + 1.0))

    sub = lax.broadcasted_iota(jnp.int32, (group, V7X_LANES), 0) % V7X_SUBLANES
    for s in (1, 2, 4):
        shift = (group - s) if reverse else s
        m = (sub < V7X_SUBLANES - s) if reverse else (sub >= s)
        q = jnp.where(m, p * pltpu.roll(q, shift, 0) + q, q)
        p = jnp.where(m, p * pltpu.roll(p, shift, 0), p)
    c_in = carry[...]
    after = p * c_in + q
    first = (sub == V7X_SUBLANES - 1) if reverse else (sub == 0)
    entering = jnp.where(first, c_in, pltpu.roll(after, (group - 1) if reverse else 1, 0))
    last = 0 if reverse else V7X_SUBLANES - 1
    carry[...] = jnp.concatenate(
        [jnp.broadcast_to(after[slab * V7X_SUBLANES + last:slab * V7X_SUBLANES + last + 1, :],
                          (V7X_SUBLANES, V7X_LANES)) for slab in range(RG_SLABS)], axis=0)
    return entering


def _rg_body(*refs, reverse):
    if reverse:
        uc_ref, wg_ref, ba_ref, bx_ref, lam_ref, hf_ref, gate_ref, o_ref, a_s, b_s, carry = refs
    else:
        uc_ref, wg_ref, ba_ref, bx_ref, lam_ref, o_ref, a_s, b_s, carry = refs

    @pl.when(pl.program_id(1) == 0)
    def _():
        carry[...] = jnp.zeros_like(carry)

    _rg_gates(uc_ref[0], wg_ref, ba_ref, bx_ref, lam_ref, a_s, b_s)
    entering = _rg_scan_tile(a_s, b_s, carry, reverse=reverse)
    for slab in range(RG_SLABS):
        lanes = slice(slab * V7X_LANES, (slab + 1) * V7X_LANES)
        for c in range(RG_CHUNKS):
            rows = slice(c * RG_CHUNK_ROWS, (c + 1) * RG_CHUNK_ROWS)
            e = entering[slab * V7X_SUBLANES + c:slab * V7X_SUBLANES + c + 1, :]
            h = a_s[_slab_rows(slab, c), :] + b_s[_slab_rows(slab, c), :] * e
            if reverse:
                o_ref[0, rows, lanes] = ((hf_ref[0, rows, lanes] + h)
                                         * gate_ref[0, rows, lanes].astype(F32)).astype(o_ref.dtype)
            else:
                o_ref[0, rows, lanes] = h


def _rg_core(uc, gate, p):
    bsz, seq, d = uc.shape
    tl = RG_TILE
    assert seq % tl == 0 and d == D_MODEL
    n_tiles = seq // tl
    row = pl.BlockSpec((1, d), lambda b, t: (0, 0))
    gate_w = pl.BlockSpec((LRU_BLOCKS, LRU_BW, 2 * LRU_BW), lambda b, t: (0, 0, 0))
    scan_scratch = [pltpu.VMEM((RG_SLABS * tl, V7X_LANES), F32), pltpu.VMEM((RG_SLABS * tl, V7X_LANES), F32),
                    pltpu.VMEM((RG_SLABS * V7X_SUBLANES, V7X_LANES), F32)]

    fwd = pl.BlockSpec((1, tl, d), lambda b, t: (b, t, 0))
    hf = pl.pallas_call(
        functools.partial(_rg_body, reverse=False),
        grid=(bsz, n_tiles),
        in_specs=[fwd, gate_w, row, row, row],
        out_specs=fwd,
        out_shape=jax.ShapeDtypeStruct((bsz, seq, d), F32),
        scratch_shapes=scan_scratch,
        compiler_params=_params(("parallel", "arbitrary")),
        name="rg_fwd",
    )(uc, p["wg"][0], p["ba"][0], p["bx"][0], p["lam"][0])

    bwd = pl.BlockSpec((1, tl, d), lambda b, t: (b, n_tiles - 1 - t, 0))
    return pl.pallas_call(
        functools.partial(_rg_body, reverse=True),
        grid=(bsz, n_tiles),
        in_specs=[bwd, gate_w, row, row, row, bwd, bwd],
        out_specs=bwd,
        out_shape=jax.ShapeDtypeStruct((bsz, seq, d), BF16),
        scratch_shapes=scan_scratch,
        compiler_params=_params(("parallel", "arbitrary")),
        name="rg_bwd",
    )(uc, p["wg"][1], p["ba"][1], p["bx"][1], p["lam"][1], hf, gate)


def _prefix_rows(x, rid):
    q = x.shape[0]
    s = 1
    while s < q:
        x = x + jnp.where(rid >= s, pltpu.roll(x, s, 0), 0.0)
        s *= 2
    return x


def _suffix_rows(x, rid):
    q = x.shape[0]
    s = 1
    while s < q:
        x = x + jnp.where(rid < q - s, pltpu.roll(x, q - s, 0), 0.0)
        s *= 2
    return x


def _expand_heads(v, r_ref, *, split=True):
    hi = v.astype(BF16)
    out = jnp.dot(hi, r_ref[...], preferred_element_type=F32)
    if split:
        out = out + jnp.dot((v - hi.astype(F32)).astype(BF16), r_ref[...], preferred_element_type=F32)
    return out


def _ssd_dt(dt_ref, dtb_ref, alog_ref):
    dt = _softplus(dt_ref[0] + dtb_ref[...])
    return dt, dt * (-jnp.exp(alog_ref[...]))


def _ssd1_body(x_ref, dt_ref, dtb_ref, alog_ref, dsk_ref, rf_ref, rb_ref, y1_ref, c_ref, sb_ref, hf_s):
    q = SSD_CHUNK

    @pl.when(pl.program_id(1) == 0)
    def _():
        hf_s[...] = jnp.zeros_like(hf_s)

    act = x_ref[0]
    xs = act[:, :SSD_D_INNER]
    bm = act[:, SSD_D_INNER:SSD_D_INNER + SSD_GROUPS * SSD_STATE]
    cm = act[:, SSD_D_INNER + SSD_GROUPS * SSD_STATE:]
    c_ref[0] = cm.astype(BF16)

    dt, dta = _ssd_dt(dt_ref, dtb_ref, alog_ref)
    rid = lax.broadcasted_iota(jnp.int32, (q, q), 0)
    cid = lax.broadcasted_iota(jnp.int32, (q, q), 1)
    is_fwd = cid < SSD_HEADS
    cum = jnp.where(is_fwd, _prefix_rows(dta, rid), _suffix_rows(dta, rid))
    tot = jnp.where(is_fwd[0:1, :], cum[q - 1:q, :], cum[0:1, :])
    e_in = jnp.exp(cum)
    w_end = dt * jnp.exp(tot - cum)

    ef_x = _expand_heads(e_in, rf_ref)
    xf = (xs * _expand_heads(w_end, rf_ref, split=False)).astype(BF16)
    xb = (xs * _expand_heads(w_end, rb_ref, split=False)).astype(BF16)

    dst = cum * LOG2E
    src_t = (dst - jnp.log2(dt)).T
    lower = rid >= cid
    upper = rid <= cid
    neg_inf = jnp.float32(-jnp.inf)
    lane_lo = cid < SSD_HEAD_DIM

    for g in range(SSD_GROUPS):
        ns = slice(g * SSD_STATE, (g + 1) * SSD_STATE)
        gs = slice(g * SSD_GROUP_W, (g + 1) * SSD_GROUP_W)
        bg = bm[:, ns]
        cg16 = cm[:, ns].astype(BF16)
        cb = lax.dot_general(cg16, bg.astype(BF16), (((1,), (1,)), ((), ())), preferred_element_type=F32)
        bgt16 = bg.T.astype(BF16)
        sf_t = jnp.dot(bgt16, xf[:, gs], preferred_element_type=F32)
        sb_ref[0, 0, :, gs] = jnp.dot(bgt16, xb[:, gs], preferred_element_type=F32).astype(sb_ref.dtype)
        h_in = hf_s[:, gs]
        y_off = jnp.dot(cg16, h_in.astype(BF16), preferred_element_type=F32) * ef_x[:, gs]
        hf_s[:, gs] = h_in * ef_x[q - 1:q, gs] + sf_t

        for pr in range(SSD_GROUP_W // (2 * SSD_HEAD_DIM)):
            h0 = g * (SSD_HEADS // SSD_GROUPS) + 2 * pr
            ms = []
            for h in (h0, h0 + 1):
                hb = SSD_HEADS + h
                d_f = dst[:, h:h + 1] - src_t[h:h + 1, :]
                d_b = dst[:, hb:hb + 1] - src_t[hb:hb + 1, :]
                m_f = jnp.exp2(jnp.where(lower, d_f, neg_inf))
                m_b = jnp.exp2(jnp.where(upper, d_b, neg_inf))
                ms.append((cb * (m_f + m_b)).astype(BF16))
            lhs = jnp.concatenate(ms, axis=1)
            ps = slice(h0 * SSD_HEAD_DIM, (h0 + 2) * SSD_HEAD_DIM)
            xp = xs[:, ps]
            rhs = jnp.concatenate([jnp.where(lane_lo, xp, 0.0).astype(BF16),
                                   jnp.where(lane_lo, 0.0, xp).astype(BF16)], axis=0)
            y_diag = jnp.dot(lhs, rhs, preferred_element_type=F32)
            os_ = slice(pr * 2 * SSD_HEAD_DIM, (pr + 1) * 2 * SSD_HEAD_DIM)
            y1_ref[0, :, ps] = (y_diag + y_off[:, os_] + xs[:, ps] * dsk_ref[:, ps]).astype(y1_ref.dtype)


def _ssd2_body(y1_ref, c_ref, sb_ref, dt_ref, zs_ref, dtb_ref, alog_ref, rb_ref, ng_ref, o_ref, hb_s):
    q = SSD_CHUNK
    c = pl.program_id(1)

    @pl.when(c == 0)
    def _():
        hb_s[...] = jnp.zeros_like(hb_s)

    _, dta = _ssd_dt(dt_ref, dtb_ref, alog_ref)
    rid = lax.broadcasted_iota(jnp.int32, (q, q), 0)
    eb_x = _expand_heads(jnp.exp(_suffix_rows(dta, rid)), rb_ref)

    for g in range(SSD_GROUPS):
        ns = slice(g * SSD_STATE, (g + 1) * SSD_STATE)
        gs = slice(g * SSD_GROUP_W, (g + 1) * SSD_GROUP_W)
        h_in = hb_s[:, gs]
        y_off = jnp.dot(c_ref[0, :, ns], h_in.astype(BF16), preferred_element_type=F32) * eb_x[:, gs]
        hb_s[:, gs] = h_in * eb_x[0:1, gs] + sb_ref[0, 0, :, gs].astype(F32)
        y = (y1_ref[0, :, gs].astype(F32) + y_off) * zs_ref[0, :, gs].astype(F32)
        ms = jnp.mean(y * y, axis=-1, keepdims=True)
        o_ref[0, :, gs] = (y * lax.rsqrt(ms + LN_EPS) * ng_ref[:, gs]).astype(BF16)


def _ssd_core(xbc_act, dt_raw, zs, dtb, alog, dskip_x, rf, rb, ng):
    bsz, seq, _ = xbc_act.shape
    q = SSD_CHUNK
    assert seq % q == 0 and q == 2 * SSD_HEADS
    nc = seq // q
    row = lambda r, w: pl.BlockSpec((r, w), lambda b, c: (0, 0))
    y1, c16, sb = pl.pallas_call(
        _ssd1_body,
        grid=(bsz, nc),
        in_specs=[
            pl.BlockSpec((1, q, SSD_CONV_DIM), lambda b, c: (b, c, 0)),
            pl.BlockSpec((1, q, 2 * SSD_HEADS), lambda b, c: (b, c, 0)),
            row(1, 2 * SSD_HEADS), row(1, 2 * SSD_HEADS),
            row(1, SSD_D_INNER), row(2 * SSD_HEADS, SSD_D_INNER), row(2 * SSD_HEADS, SSD_D_INNER),
        ],
        out_specs=[
            pl.BlockSpec((1, q, SSD_D_INNER), lambda b, c: (b, c, 0)),
            pl.BlockSpec((1, q, SSD_GROUPS * SSD_STATE), lambda b, c: (b, c, 0)),
            pl.BlockSpec((1, 1, SSD_STATE, SSD_D_INNER), lambda b, c: (b, c, 0, 0)),
        ],
        out_shape=[
            jax.ShapeDtypeStruct((bsz, seq, SSD_D_INNER), BF16),
            jax.ShapeDtypeStruct((bsz, seq, SSD_GROUPS * SSD_STATE), BF16),
            jax.ShapeDtypeStruct((bsz, nc, SSD_STATE, SSD_D_INNER), BF16),
        ],
        scratch_shapes=[pltpu.VMEM((SSD_STATE, SSD_D_INNER), F32)],
        compiler_params=_params(("parallel", "arbitrary")),
        name="ssd_fwd",
    )(xbc_act, dt_raw, dtb, alog, dskip_x, rf, rb)

    rev = lambda b, c: (b, nc - 1 - c, 0)
    return pl.pallas_call(
        _ssd2_body,
        grid=(bsz, nc),
        in_specs=[
            pl.BlockSpec((1, q, SSD_D_INNER), rev),
            pl.BlockSpec((1, q, SSD_GROUPS * SSD_STATE), rev),
            pl.BlockSpec((1, 1, SSD_STATE, SSD_D_INNER), lambda b, c: (b, nc - 1 - c, 0, 0)),
            pl.BlockSpec((1, q, 2 * SSD_HEADS), rev),
            pl.BlockSpec((1, q, SSD_D_INNER), rev),
            row(1, 2 * SSD_HEADS), row(1, 2 * SSD_HEADS), row(2 * SSD_HEADS, SSD_D_INNER), row(1, SSD_D_INNER),
        ],
        out_specs=pl.BlockSpec((1, q, SSD_D_INNER), rev),
        out_shape=jax.ShapeDtypeStruct((bsz, seq, SSD_D_INNER), BF16),
        scratch_shapes=[pltpu.VMEM((SSD_STATE, SSD_D_INNER), F32)],
        compiler_params=_params(("parallel", "arbitrary")),
        name="ssd_bwd",
    )(y1, c16, sb, dt_raw, zs, dtb, alog, rb, ng)


NA_KEYS = WIN_H * GRID_W


NA_ROWS_PER_STEP = 32


def _na_body(q_ref, k_ref, v_ref, bias_ref, o_ref, *, rows):
    scale = NA_HEAD_DIM ** -0.5

    def step(i, carry):
        qss, kss, scores, probs = [], [], [], []
        for j in range(NA_ROWS_PER_STEP):
            r = i * NA_ROWS_PER_STEP + j
            rs = jnp.clip(r - WIN_H // 2, 0, rows - WIN_H)
            dy0 = rs - r + (WIN_H - 1)
            qs = pl.ds(pl.multiple_of(r * GRID_W, GRID_W), GRID_W)
            ks = pl.ds(pl.multiple_of(rs * GRID_W, GRID_W), NA_KEYS)
            s = lax.dot_general(q_ref[0, qs, :], k_ref[0, ks, :], (((1,), (1,)), ((), ())),
                                preferred_element_type=F32)
            scores.append(s * scale + bias_ref[0, dy0])
            qss.append(qs)
            kss.append(ks)
        for s in scores:
            p = jnp.exp(s - jnp.max(s, axis=-1, keepdims=True))
            probs.append((p.astype(BF16), 1.0 / jnp.sum(p, axis=-1, keepdims=True)))
        for qs, ks, (p, inv) in zip(qss, kss, probs):
            o = jnp.dot(p, v_ref[0, ks, :], preferred_element_type=F32)
            o_ref[0, qs, :] = (o * inv).astype(o_ref.dtype)
        return carry

    lax.fori_loop(0, rows // NA_ROWS_PER_STEP, step, 0)


def _na_bias_table(rpb):
    qc = np.arange(GRID_W)[:, None]
    kc = np.arange(GRID_W)[None, :]
    wstart = np.clip(qc - WIN_W // 2, 0, GRID_W - WIN_W)
    ok = (kc >= wstart) & (kc < wstart + WIN_W)
    dx = np.clip(kc - qc + WIN_W - 1, 0, 2 * WIN_W - 2)
    per_dy = jnp.where(ok[None, None], rpb[:, :, dx].astype(F32), -jnp.inf)
    dy = np.arange(WIN_H)[:, None] + np.arange(WIN_H)[None, :]
    tab = per_dy[:, dy]
    return jnp.transpose(tab, (0, 1, 3, 2, 4)).reshape(NA_HEADS, WIN_H, GRID_W, NA_KEYS)


def _natten(qkv, bias_tab):
    bsz, seq, _ = qkv.shape
    rows = seq // GRID_W
    assert seq % GRID_W == 0 and rows >= WIN_H and rows % NA_ROWS_PER_STEP == 0
    blk = lambda off: pl.BlockSpec((1, seq, NA_HEAD_DIM), lambda b, h: (b, 0, off + h))
    return pl.pallas_call(
        functools.partial(_na_body, rows=rows),
        grid=(bsz, NA_HEADS),
        in_specs=[blk(0), blk(NA_HEADS), blk(2 * NA_HEADS),
                  pl.BlockSpec((1, WIN_H, GRID_W, NA_KEYS), lambda b, h: (h, 0, 0, 0))],
        out_specs=blk(0),
        out_shape=jax.ShapeDtypeStruct((bsz, seq, D_MODEL), BF16),
        compiler_params=_params(("parallel", "parallel")),
        name="natten",
    )(qkv, qkv, qkv, bias_tab)


def _row(v):
    return v.reshape(1, -1).astype(F32)


def _rg_layer(x, shape, p):
    bsz, seq = shape
    gate = _matmul(x, p["w_gate"], act="gelu", out_dtype=BF16, bn=2048, name="rg_in_gate")
    uc = _matmul_conv(x.reshape(bsz, seq, D_MODEL), p["w_u"], p["cw"], p["cb"], name="rg_in_u_conv")
    y = _rg_core(uc, gate.reshape(bsz, seq, D_MODEL), p)
    return y.reshape(bsz * seq, D_MODEL), p["w_out"]


def _ssd_layer(x, shape, p):
    bsz, seq = shape
    zs = _matmul(x, p["w_z"], act="silu", out_dtype=BF16, bn=2048, name="ssd_in_z").reshape(bsz, seq, SSD_D_INNER)
    xbc = _matmul_conv(x.reshape(bsz, seq, D_MODEL), p["w_xbc"], p["cw"], p["cb"], act="silu", name="ssd_in_xbc_conv")
    dt = _matmul(x, p["w_dt"], bn=2 * SSD_HEADS, name="ssd_in_dt").reshape(bsz, seq, 2 * SSD_HEADS)
    y = _ssd_core(xbc, dt, zs, p["dtb"], p["alog"], p["dskip_x"], p["rf"], p["rb"], p["ng"])
    return y.reshape(bsz * seq, SSD_D_INNER), p["w_out"]


def _na_layer(x, shape, p):
    bsz, seq = shape
    qkv = _matmul(x, p["w_qkv"], p["b_qkv"], out_dtype=BF16, bn=2048, name="na_qkv").reshape(bsz, seq, 3 * D_MODEL)
    o = _natten(qkv, p["bias_tab"])
    return o.reshape(bsz * seq, D_MODEL), p["w_out"]


def _head_expand_matrix(offset):
    r = np.zeros((2 * SSD_HEADS, SSD_D_INNER), np.float32)
    for h in range(SSD_HEADS):
        r[offset + h, h * SSD_HEAD_DIM:(h + 1) * SSD_HEAD_DIM] = 1.0
    return jnp.asarray(r, BF16)


def _rg_params(j, w_in16, conv_w, conv_b, w_a, b_a, w_x, b_x, lam, w_out16):
    return dict(w_gate=_Weight(w_in16, j, 0, D_MODEL), w_u=_Weight(w_in16, j, D_MODEL, D_MODEL),
                cw=conv_w[j], cb=_row(conv_b[j]),
                wg=jnp.concatenate([w_a[j], w_x[j]], axis=-1).astype(BF16),
                ba=b_a[j].reshape(2, 1, D_MODEL), bx=b_x[j].reshape(2, 1, D_MODEL),
                lam=lam[j].reshape(2, 1, D_MODEL), w_out=_Weight(w_out16, j, 0, D_MODEL))


def _ssd_params(j, w_in16, conv_w, conv_b, dt_bias, a_log, d_skip, norm_g, w_out16):
    return dict(w_z=_Weight(w_in16, j, 0, SSD_D_INNER),
                w_xbc=_Weight(w_in16, j, SSD_D_INNER, SSD_CONV_DIM),
                w_dt=_Weight(w_in16, j, SSD_D_INNER + SSD_CONV_DIM, 2 * SSD_HEADS),
                cw=conv_w[j], cb=_row(conv_b[j]), dtb=_row(dt_bias[j]), alog=_row(a_log[j]),
                dskip_x=_row(jnp.repeat(d_skip[j], SSD_HEAD_DIM)),
                rf=_head_expand_matrix(0), rb=_head_expand_matrix(SSD_HEADS),
                ng=_row(norm_g[j]), w_out=_Weight(w_out16, j, 0, D_MODEL))


def _na_params(j, w_qkv16, b_qkv, rpb, w_out16):
    return dict(w_qkv=_Weight(w_qkv16, j, 0, 3 * D_MODEL), b_qkv=_row(b_qkv[j]), bias_tab=_na_bias_table(rpb[j]),
                w_out=_Weight(w_out16, j, 0, D_MODEL))


def kernel(x_prompt, x_sample, rg_w_in, rg_conv_w, rg_conv_b, rg_w_a, rg_b_a, rg_w_x, rg_b_x, rg_lambda, rg_w_out,
           ssd_w_in, ssd_conv_w, ssd_conv_b, ssd_dt_bias, ssd_a_log, ssd_d, ssd_norm_g, ssd_w_out,
           na_w_qkv, na_b_qkv, na_rpb, na_w_out, mlp_w_up, mlp_w_down, ln1_g, ln1_b, ln2_g, ln2_b):
    cast = lambda w: w.astype(BF16)
    rg_w_in16, rg_w_out16, ssd_w_in16, ssd_w_out16 = cast(rg_w_in), cast(rg_w_out), cast(ssd_w_in), cast(ssd_w_out)
    na_w_qkv16, na_w_out16, mlp_w_up16, mlp_w_down16 = cast(na_w_qkv), cast(na_w_out), cast(mlp_w_up), cast(mlp_w_down)
    layers = []
    for i in range(DEPTH):
        kind, j = i % 3, i // 3
        if kind == 0:
            fn, p = _rg_layer, _rg_params(j, rg_w_in16, rg_conv_w, rg_conv_b, rg_w_a, rg_b_a, rg_w_x, rg_b_x,
                                          rg_lambda, rg_w_out16)
        elif kind == 1:
            fn, p = _ssd_layer, _ssd_params(j, ssd_w_in16, ssd_conv_w, ssd_conv_b, ssd_dt_bias, ssd_a_log, ssd_d,
                                            ssd_norm_g, ssd_w_out16)
        else:
            fn, p = _na_layer, _na_params(j, na_w_qkv16, na_b_qkv, na_rpb, na_w_out16)
        layers.append((fn, p, _Weight(mlp_w_up16, i, 0, MLP_HIDDEN), _Weight(mlp_w_down16, i, 0, D_MODEL),
                       _row(ln1_g[i]), _row(ln1_b[i]), _row(ln2_g[i]), _row(ln2_b[i])))

    def trunk(x):
        bsz, seq, _ = x.shape
        x = x.reshape(bsz * seq, D_MODEL)
        for fn, p, w_up, w_down, g1, b1, g2, b2 in layers:
            y, w_out = fn(x, (bsz, seq), p)
            x = _matmul_ln(y, w_out, x, g1, b1, name="mixer_out_ln")
            hid = _matmul(x, w_up, act="relu2", out_dtype=BF16, bn=2048, name="mlp_up")
            x = _matmul_ln(hid, w_down, x, g2, b2, name="mlp_down_ln")
        return x.reshape(bsz, seq, D_MODEL)

    return trunk(x_prompt), trunk(x_sample)
```

```python
import functools
import math
from typing import NamedTuple

import numpy as np
import jax
import jax.numpy as jnp
from jax import lax
from jax.experimental import pallas as pl
from jax.experimental.pallas import tpu as pltpu

F32 = jnp.float32
BF16 = jnp.bfloat16

D_MODEL = 2048
DEPTH = 4
ALPHA = (2 * DEPTH) ** 0.25
LN_EPS = 1e-5
CONV_W = 4
LRU_BLOCKS = 8
LRU_BW = D_MODEL // LRU_BLOCKS
LRU_C = 8.0
SSD_D_INNER = 2 * D_MODEL
SSD_HEAD_DIM = 64
SSD_HEADS = SSD_D_INNER // SSD_HEAD_DIM
SSD_GROUPS = 8
SSD_STATE = 128
SSD_CHUNK = 128
SSD_GROUP_W = SSD_D_INNER // SSD_GROUPS
SSD_CONV_DIM = SSD_D_INNER + 2 * SSD_GROUPS * SSD_STATE
GRID_W = 64
WIN_H = 8
WIN_W = 16
NA_HEAD_DIM = 128
NA_HEADS = D_MODEL // NA_HEAD_DIM
MLP_HIDDEN = 4 * D_MODEL

V7X_SUBLANES = 8
V7X_LANES = 128
V7X_VMEM_LIMIT_BYTES = 56 * 1024 * 1024


class _Weight(NamedTuple):
    stack: jax.Array
    layer: int
    col_off: int
    n: int

    def spec(self, rows, bn, index):
        assert self.col_off % bn == 0 and self.n % bn == 0
        off = self.col_off // bn
        return pl.BlockSpec((None, rows, bn), lambda *g: (self.layer, index(*g)[0], off + index(*g)[1]))

    def resident_spec(self):
        k = self.stack.shape[1]
        assert self.col_off % self.n == 0
        return pl.BlockSpec((None, k, self.n), lambda *g: (self.layer, 0, self.col_off // self.n),
                            pipeline_mode=pl.Buffered(1))


def _params(semantics, vmem_bytes=V7X_VMEM_LIMIT_BYTES):
    return pltpu.CompilerParams(dimension_semantics=semantics, vmem_limit_bytes=vmem_bytes)


def _sigmoid(x):
    return 1.0 / (1.0 + jnp.exp(-x))


def _softplus(x):
    return jnp.maximum(x, 0.0) + jnp.log1p(jnp.exp(-jnp.abs(x)))


def _gelu_tanh(x):
    c = math.sqrt(2.0 / math.pi)
    return 0.5 * x * (1.0 + jnp.tanh(c * (x + 0.044715 * (x * x * x))))


def _relu2(x):
    r = jnp.maximum(x, 0.0)
    return r * r


def _silu(x):
    return x * _sigmoid(x)


_ACTS = {"none": lambda v: v, "gelu": _gelu_tanh, "relu2": _relu2, "silu": _silu}


def _mm_body(*refs, act, has_bias):
    if has_bias:
        x_ref, w_ref, b_ref, o_ref = refs
    else:
        x_ref, w_ref, o_ref = refs
    acc = jnp.dot(x_ref[...].astype(BF16), w_ref[...], preferred_element_type=F32)
    if has_bias:
        acc = acc + b_ref[...]
    o_ref[...] = _ACTS[act](acc).astype(o_ref.dtype)


def _matmul(x, w, bias=None, *, act="none", out_dtype=F32, bm=1024, bn=1024, name="mm"):
    m, k = x.shape
    n = w.n
    assert k == w.stack.shape[1] and m % bm == 0, (x.shape, w.stack.shape, bm, bn)
    in_specs = [pl.BlockSpec((bm, k), lambda i, j: (i, 0)),
                w.spec(k, bn, lambda i, j: (0, j))]
    args = [x, w.stack]
    if bias is not None:
        in_specs.append(pl.BlockSpec((1, bn), lambda i, j: (0, j)))
        args.append(bias)
    return pl.pallas_call(
        functools.partial(_mm_body, act=act, has_bias=bias is not None),
        grid=(m // bm, n // bn),
        in_specs=in_specs,
        out_specs=pl.BlockSpec((bm, bn), lambda i, j: (i, j)),
        out_shape=jax.ShapeDtypeStruct((m, n), out_dtype),
        compiler_params=_params(("parallel", "parallel")),
        name=name,
    )(*args)


CONV_HALO = 16
CONV_SUBDOT = 256


def _mm_conv_body(x_ref, xp_ref, xn_ref, w_ref, cw_ref, cb_ref, o_ref, *, act, n_tiles):
    t = pl.program_id(1)
    rows = x_ref.shape[1]
    n = rows + 2 * CONV_HALO
    lo, hi = CONV_HALO, CONV_HALO + rows
    ext = jnp.concatenate([xp_ref[0] * (t > 0).astype(xp_ref.dtype), x_ref[0],
                           xn_ref[0] * (t < n_tiles - 1).astype(xn_ref.dtype)], axis=0).astype(BF16)
    for c in range(o_ref.shape[2] // CONV_SUBDOT):
        cols = slice(c * CONV_SUBDOT, (c + 1) * CONV_SUBDOT)
        y = jnp.dot(ext, w_ref[:, cols], preferred_element_type=F32)
        v = (cb_ref[:, cols] + pltpu.roll(y, 1, 0)[lo:hi] * cw_ref[0:1, cols] + y[lo:hi] * cw_ref[1:2, cols]
             + pltpu.roll(y, n - 1, 0)[lo:hi] * cw_ref[2:3, cols] + pltpu.roll(y, n - 2, 0)[lo:hi] * cw_ref[3:4, cols])
        o_ref[0, :, cols] = _ACTS[act](v).astype(o_ref.dtype)


def _matmul_conv(x, w, cw, cb, *, act="none", tl=1024, bn=1024, name="mm_conv"):
    bsz, seq, k = x.shape
    n = w.n
    assert seq % tl == 0 and bn % CONV_SUBDOT == 0 and tl % CONV_HALO == 0 and k == w.stack.shape[1]
    n_tiles = seq // tl
    hb = tl // CONV_HALO
    n_hblk = seq // CONV_HALO
    return pl.pallas_call(
        functools.partial(_mm_conv_body, act=act, n_tiles=n_tiles),
        grid=(bsz, n_tiles, n // bn),
        in_specs=[pl.BlockSpec((1, tl, k), lambda b, t, j: (b, t, 0)),
                  pl.BlockSpec((1, CONV_HALO, k), lambda b, t, j: (b, jnp.maximum(t * hb - 1, 0), 0)),
                  pl.BlockSpec((1, CONV_HALO, k), lambda b, t, j: (b, jnp.minimum((t + 1) * hb, n_hblk - 1), 0)),
                  w.spec(k, bn, lambda b, t, j: (0, j)),
                  pl.BlockSpec((CONV_W, bn), lambda b, t, j: (0, j)),
                  pl.BlockSpec((1, bn), lambda b, t, j: (0, j))],
        out_specs=pl.BlockSpec((1, tl, bn), lambda b, t, j: (b, t, j)),
        out_shape=jax.ShapeDtypeStruct((bsz, seq, n), F32),
        compiler_params=_params(("parallel", "parallel", "parallel")),
        name=name,
    )(x, x, x, w.stack, cw, cb)


MM_LN_TILES = {D_MODEL: (1024, 4), SSD_D_INNER: (512, 2), MLP_HIDDEN: (256, 2)}


def _mm_ln_body(x_ref, w_ref, res_ref, g_ref, b_ref, o_ref, *, split):
    hm = x_ref.shape[0] // split
    for h in range(split):
        rows = slice(h * hm, (h + 1) * hm)
        z = ALPHA * res_ref[rows, :] + jnp.dot(x_ref[rows, :], w_ref[...], preferred_element_type=F32)
        mu = jnp.mean(z, axis=-1, keepdims=True)
        zc = z - mu
        var = jnp.mean(zc * zc, axis=-1, keepdims=True)
        o_ref[rows, :] = zc * lax.rsqrt(var + LN_EPS) * g_ref[...] + b_ref[...]


def _matmul_ln(x, w, res, g, b, *, name="mm_ln"):
    m, k = x.shape
    n = w.n
    bm, split = MM_LN_TILES[k]
    assert k == w.stack.shape[1] and n == D_MODEL and m % bm == 0 and bm % split == 0
    return pl.pallas_call(
        functools.partial(_mm_ln_body, split=split),
        grid=(m // bm,),
        in_specs=[pl.BlockSpec((bm, k), lambda i: (i, 0)),
                  w.resident_spec(),
                  pl.BlockSpec((bm, n), lambda i: (i, 0)),
                  pl.BlockSpec((1, n), lambda i: (0, 0)),
                  pl.BlockSpec((1, n), lambda i: (0, 0))],
        out_specs=pl.BlockSpec((bm, n), lambda i: (i, 0)),
        out_shape=jax.ShapeDtypeStruct((m, n), F32),
        compiler_params=_params(("parallel",)),
        name=name,
    )(x, w.stack, res, g, b)


RG_TILE = 512
RG_CHUNKS = V7X_SUBLANES
RG_CHUNK_ROWS = RG_TILE // RG_CHUNKS
RG_SLABS = D_MODEL // V7X_LANES
LOG2E = math.log2(math.e)
SQRT_FLOOR = 1e-30


def _sigmoid_tanh(x):
    return 0.5 * jnp.tanh(0.5 * x) + 0.5


def _slab_rows(slab, chunk):
    return pl.ds(slab * RG_TILE + chunk, RG_CHUNK_ROWS, stride=RG_CHUNKS)


def _rg_gates(uc, wg_ref, ba_ref, bx_ref, lam_ref, a_s, b_s):
    nsp2 = (-LRU_C * LOG2E) * _softplus(-lam_ref[...])
    for n in range(LRU_BLOCKS):
        sl = slice(n * LRU_BW, (n + 1) * LRU_BW)
        un = uc[:, sl]
        gates = jnp.dot(un.astype(BF16), wg_ref[n], preferred_element_type=F32)
        r = _sigmoid_tanh(gates[:, :LRU_BW] + ba_ref[:, sl])
        i = _sigmoid_tanh(gates[:, LRU_BW:] + bx_ref[:, sl])
        a = jnp.exp2(r * nsp2[:, sl])
        v = 1.0 - a * a
        b = (v * lax.rsqrt(jnp.maximum(v, SQRT_FLOOR))) * (i * un)
        for half in range(LRU_BW // V7X_LANES):
            slab = n * (LRU_BW // V7X_LANES) + half
            lanes = slice(half * V7X_LANES, (half + 1) * V7X_LANES)
            for c in range(RG_CHUNKS):
                rows = slice(c * RG_CHUNK_ROWS, (c + 1) * RG_CHUNK_ROWS)
                a_s[_slab_rows(slab, c), :] = a[rows, lanes]
                b_s[_slab_rows(slab, c), :] = b[rows, lanes]


def _rg_scan_tile(a_s, b_s, carry, *, reverse):
    group = RG_SLABS * V7X_SUBLANES

    def step(i, hp):
        h, p = hp
        t = (RG_CHUNK_ROWS - 1 - i) if reverse else i
        rows = [pl.ds(pl.multiple_of(slab * RG_TILE + t * V7X_SUBLANES, V7X_SUBLANES), V7X_SUBLANES)
                for slab in range(RG_SLABS)]
        a = jnp.concatenate([a_s[r, :] for r in rows], axis=0)
        b = jnp.concatenate([b_s[r, :] for r in rows], axis=0)
        h = a * h + b
        p = p * a
        for slab, r in enumerate(rows):
            a_s[r, :] = h[slab * V7X_SUBLANES:(slab + 1) * V7X_SUBLANES]
            b_s[r, :] = p[slab * V7X_SUBLANES:(slab + 1) * V7X_SUBLANES]
        return h, p

    zeros = jnp.zeros((group, V7X_LANES), F32)
    q, p = lax.fori_loop(0, RG_CHUNK_ROWS, step, (zeros, zeros + 1.0))

    sub = lax.broadcasted_iota(jnp.int32, (group, V7X_LANES), 0) % V7X_SUBLANES
    for s in (1, 2, 4):
        shift = (group - s) if reverse else s
        m = (sub < V7X_SUBLANES - s) if reverse else (sub >= s)
        q = jnp.where(m, p * pltpu.roll(q, shift, 0) + q, q)
        p = jnp.where(m, p * pltpu.roll(p, shift, 0), p)
    c_in = carry[...]
    after = p * c_in + q
    first = (sub == V7X_SUBLANES - 1) if reverse else (sub == 0)
    entering = jnp.where(first, c_in, pltpu.roll(after, (group - 1) if reverse else 1, 0))
    last = 0 if reverse else V7X_SUBLANES - 1
    carry[...] = jnp.concatenate(
        [jnp.broadcast_to(after[slab * V7X_SUBLANES + last:slab * V7X_SUBLANES + last + 1, :],
                          (V7X_SUBLANES, V7X_LANES)) for slab in range(RG_SLABS)], axis=0)
    return entering


def _rg_body(*refs, reverse):
    if reverse:
        uc_ref, wg_ref, ba_ref, bx_ref, lam_ref, hf_ref, gate_ref, o_ref, a_s, b_s, carry = refs
    else:
        uc_ref, wg_ref, ba_ref, bx_ref, lam_ref, o_ref, a_s, b_s, carry = refs

    @pl.when(pl.program_id(1) == 0)
    def _():
        carry[...] = jnp.zeros_like(carry)

    _rg_gates(uc_ref[0], wg_ref, ba_ref, bx_ref, lam_ref, a_s, b_s)
    entering = _rg_scan_tile(a_s, b_s, carry, reverse=reverse)
    for slab in range(RG_SLABS):
        lanes = slice(slab * V7X_LANES, (slab + 1) * V7X_LANES)
        tile_rows = slice(slab * RG_TILE, (slab + 1) * RG_TILE)
        e = jnp.tile(entering[slab * V7X_SUBLANES:(slab + 1) * V7X_SUBLANES, :], (RG_CHUNK_ROWS, 1))
        h = a_s[tile_rows, :] + b_s[tile_rows, :] * e
        if not reverse:
            o_ref[0, :, lanes] = h
            continue
        a_s[tile_rows, :] = hf_ref[0, :, lanes] + h
        for c in range(RG_CHUNKS):
            rows = slice(c * RG_CHUNK_ROWS, (c + 1) * RG_CHUNK_ROWS)
            o_ref[0, rows, lanes] = (a_s[_slab_rows(slab, c), :]
                                     * gate_ref[0, rows, lanes].astype(F32)).astype(o_ref.dtype)


def _rg_core(uc, gate, p):
    bsz, seq, d = uc.shape
    tl = RG_TILE
    assert seq % tl == 0 and d == D_MODEL
    n_tiles = seq // tl
    row = pl.BlockSpec((1, d), lambda b, t: (0, 0))
    gate_w = pl.BlockSpec((LRU_BLOCKS, LRU_BW, 2 * LRU_BW), lambda b, t: (0, 0, 0))
    scan_scratch = [pltpu.VMEM((RG_SLABS * tl, V7X_LANES), F32), pltpu.VMEM((RG_SLABS * tl, V7X_LANES), F32),
                    pltpu.VMEM((RG_SLABS * V7X_SUBLANES, V7X_LANES), F32)]

    fwd = pl.BlockSpec((1, tl, d), lambda b, t: (b, t, 0))
    hf = pl.pallas_call(
        functools.partial(_rg_body, reverse=False),
        grid=(bsz, n_tiles),
        in_specs=[fwd, gate_w, row, row, row],
        out_specs=fwd,
        out_shape=jax.ShapeDtypeStruct((bsz, seq, d), F32),
        scratch_shapes=scan_scratch,
        compiler_params=_params(("parallel", "arbitrary")),
        name="rg_fwd",
    )(uc, p["wg"][0], p["ba"][0], p["bx"][0], p["lam"][0])

    bwd = pl.BlockSpec((1, tl, d), lambda b, t: (b, n_tiles - 1 - t, 0))
    return pl.pallas_call(
        functools.partial(_rg_body, reverse=True),
        grid=(bsz, n_tiles),
        in_specs=[bwd, gate_w, row, row, row, bwd, bwd],
        out_specs=bwd,
        out_shape=jax.ShapeDtypeStruct((bsz, seq, d), BF16),
        scratch_shapes=scan_scratch,
        compiler_params=_params(("parallel", "arbitrary")),
        name="rg_bwd",
    )(uc, p["wg"][1], p["ba"][1], p["bx"][1], p["lam"][1], hf, gate)


def _prefix_rows(x, rid):
    q = x.shape[0]
    s = 1
    while s < q:
        x = x + jnp.where(rid >= s, pltpu.roll(x, s, 0), 0.0)
        s *= 2
    return x


def _suffix_rows(x, rid):
    q = x.shape[0]
    s = 1
    while s < q:
        x = x + jnp.where(rid < q - s, pltpu.roll(x, q - s, 0), 0.0)
        s *= 2
    return x


def _expand_heads(v, r_ref, *, split=True):
    hi = v.astype(BF16)
    out = jnp.dot(hi, r_ref[...], preferred_element_type=F32)
    if split:
        out = out + jnp.dot((v - hi.astype(F32)).astype(BF16), r_ref[...], preferred_element_type=F32)
    return out


def _ssd_dt(dt_ref, dtb_ref, alog_ref):
    dt = _softplus(dt_ref[0] + dtb_ref[...])
    return dt, dt * (-jnp.exp(alog_ref[...]))


def _ssd1_body(x_ref, dt_ref, dtb_ref, alog_ref, dsk_ref, rf_ref, rb_ref, y1_ref, c_ref, sb_ref, hf_s):
    q = SSD_CHUNK

    @pl.when(pl.program_id(1) == 0)
    def _():
        hf_s[...] = jnp.zeros_like(hf_s)

    act = x_ref[0]
    xs = act[:, :SSD_D_INNER]
    bm = act[:, SSD_D_INNER:SSD_D_INNER + SSD_GROUPS * SSD_STATE]
    cm = act[:, SSD_D_INNER + SSD_GROUPS * SSD_STATE:]
    c_ref[0] = cm.astype(BF16)

    dt, dta = _ssd_dt(dt_ref, dtb_ref, alog_ref)
    rid = lax.broadcasted_iota(jnp.int32, (q, q), 0)
    cid = lax.broadcasted_iota(jnp.int32, (q, q), 1)
    is_fwd = cid < SSD_HEADS
    cum = jnp.where(is_fwd, _prefix_rows(dta, rid), _suffix_rows(dta, rid))
    tot = jnp.where(is_fwd[0:1, :], cum[q - 1:q, :], cum[0:1, :])
    e_in = jnp.exp(cum)
    w_end = dt * jnp.exp(tot - cum)

    ef_x = _expand_heads(e_in, rf_ref)
    xf = (xs * _expand_heads(w_end, rf_ref, split=False)).astype(BF16)
    xb = (xs * _expand_heads(w_end, rb_ref, split=False)).astype(BF16)

    dst = cum * LOG2E
    src_t = (dst - jnp.log2(dt)).T
    lower = rid >= cid
    upper = rid <= cid
    neg_inf = jnp.float32(-jnp.inf)
    lane_lo = cid < SSD_HEAD_DIM

    for g in range(SSD_GROUPS):
        ns = slice(g * SSD_STATE, (g + 1) * SSD_STATE)
        gs = slice(g * SSD_GROUP_W, (g + 1) * SSD_GROUP_W)
        bg = bm[:, ns]
        cg16 = cm[:, ns].astype(BF16)
        cb = lax.dot_general(cg16, bg.astype(BF16), (((1,), (1,)), ((), ())), preferred_element_type=F32)
        bgt16 = bg.T.astype(BF16)
        sf_t = jnp.dot(bgt16, xf[:, gs], preferred_element_type=F32)
        sb_ref[0, 0, :, gs] = jnp.dot(bgt16, xb[:, gs], preferred_element_type=F32).astype(sb_ref.dtype)
        h_in = hf_s[:, gs]
        y_off = jnp.dot(cg16, h_in.astype(BF16), preferred_element_type=F32) * ef_x[:, gs]
        hf_s[:, gs] = h_in * ef_x[q - 1:q, gs] + sf_t

        for pr in range(SSD_GROUP_W // (2 * SSD_HEAD_DIM)):
            h0 = g * (SSD_HEADS // SSD_GROUPS) + 2 * pr
            ms = []
            for h in (h0, h0 + 1):
                hb = SSD_HEADS + h
                d_f = dst[:, h:h + 1] - src_t[h:h + 1, :]
                d_b = dst[:, hb:hb + 1] - src_t[hb:hb + 1, :]
                m_f = jnp.exp2(jnp.where(lower, d_f, neg_inf))
                m_b = jnp.exp2(jnp.where(upper, d_b, neg_inf))
                ms.append((cb * (m_f + m_b)).astype(BF16))
            lhs = jnp.concatenate(ms, axis=1)
            ps = slice(h0 * SSD_HEAD_DIM, (h0 + 2) * SSD_HEAD_DIM)
            xp = xs[:, ps]
            rhs = jnp.concatenate([jnp.where(lane_lo, xp, 0.0).astype(BF16),
                                   jnp.where(lane_lo, 0.0, xp).astype(BF16)], axis=0)
            y_diag = jnp.dot(lhs, rhs, preferred_element_type=F32)
            os_ = slice(pr * 2 * SSD_HEAD_DIM, (pr + 1) * 2 * SSD_HEAD_DIM)
            y1_ref[0, :, ps] = (y_diag + y_off[:, os_] + xs[:, ps] * dsk_ref[:, ps]).astype(y1_ref.dtype)


def _ssd2_body(y1_ref, c_ref, sb_ref, dt_ref, zs_ref, dtb_ref, alog_ref, rb_ref, ng_ref, o_ref, hb_s):
    q = SSD_CHUNK
    c = pl.program_id(1)

    @pl.when(c == 0)
    def _():
        hb_s[...] = jnp.zeros_like(hb_s)

    _, dta = _ssd_dt(dt_ref, dtb_ref, alog_ref)
    rid = lax.broadcasted_iota(jnp.int32, (q, q), 0)
    eb_x = _expand_heads(jnp.exp(_suffix_rows(dta, rid)), rb_ref)

    for g in range(SSD_GROUPS):
        ns = slice(g * SSD_STATE, (g + 1) * SSD_STATE)
        gs = slice(g * SSD_GROUP_W, (g + 1) * SSD_GROUP_W)
        h_in = hb_s[:, gs]
        y_off = jnp.dot(c_ref[0, :, ns], h_in.astype(BF16), preferred_element_type=F32) * eb_x[:, gs]
        hb_s[:, gs] = h_in * eb_x[0:1, gs] + sb_ref[0, 0, :, gs].astype(F32)
        y = (y1_ref[0, :, gs].astype(F32) + y_off) * zs_ref[0, :, gs].astype(F32)
        ms = jnp.mean(y * y, axis=-1, keepdims=True)
        o_ref[0, :, gs] = (y * lax.rsqrt(ms + LN_EPS) * ng_ref[:, gs]).astype(BF16)


def _ssd_core(xbc_act, dt_raw, zs, dtb, alog, dskip_x, rf, rb, ng):
    bsz, seq, _ = xbc_act.shape
    q = SSD_CHUNK
    assert seq % q == 0 and q == 2 * SSD_HEADS
    nc = seq // q
    row = lambda r, w: pl.BlockSpec((r, w), lambda b, c: (0, 0))
    y1, c16, sb = pl.pallas_call(
        _ssd1_body,
        grid=(bsz, nc),
        in_specs=[
            pl.BlockSpec((1, q, SSD_CONV_DIM), lambda b, c: (b, c, 0)),
            pl.BlockSpec((1, q, 2 * SSD_HEADS), lambda b, c: (b, c, 0)),
            row(1, 2 * SSD_HEADS), row(1, 2 * SSD_HEADS),
            row(1, SSD_D_INNER), row(2 * SSD_HEADS, SSD_D_INNER), row(2 * SSD_HEADS, SSD_D_INNER),
        ],
        out_specs=[
            pl.BlockSpec((1, q, SSD_D_INNER), lambda b, c: (b, c, 0)),
            pl.BlockSpec((1, q, SSD_GROUPS * SSD_STATE), lambda b, c: (b, c, 0)),
            pl.BlockSpec((1, 1, SSD_STATE, SSD_D_INNER), lambda b, c: (b, c, 0, 0)),
        ],
        out_shape=[
            jax.ShapeDtypeStruct((bsz, seq, SSD_D_INNER), BF16),
            jax.ShapeDtypeStruct((bsz, seq, SSD_GROUPS * SSD_STATE), BF16),
            jax.ShapeDtypeStruct((bsz, nc, SSD_STATE, SSD_D_INNER), BF16),
        ],
        scratch_shapes=[pltpu.VMEM((SSD_STATE, SSD_D_INNER), F32)],
        compiler_params=_params(("parallel", "arbitrary")),
        name="ssd_fwd",
    )(xbc_act, dt_raw, dtb, alog, dskip_x, rf, rb)

    rev = lambda b, c: (b, nc - 1 - c, 0)
    return pl.pallas_call(
        _ssd2_body,
        grid=(bsz, nc),
        in_specs=[
            pl.BlockSpec((1, q, SSD_D_INNER), rev),
            pl.BlockSpec((1, q, SSD_GROUPS * SSD_STATE), rev),
            pl.BlockSpec((1, 1, SSD_STATE, SSD_D_INNER), lambda b, c: (b, nc - 1 - c, 0, 0)),
            pl.BlockSpec((1, q, 2 * SSD_HEADS), rev),
            pl.BlockSpec((1, q, SSD_D_INNER), rev),
            row(1, 2 * SSD_HEADS), row(1, 2 * SSD_HEADS), row(2 * SSD_HEADS, SSD_D_INNER), row(1, SSD_D_INNER),
        ],
        out_specs=pl.BlockSpec((1, q, SSD_D_INNER), rev),
        out_shape=jax.ShapeDtypeStruct((bsz, seq, SSD_D_INNER), BF16),
        scratch_shapes=[pltpu.VMEM((SSD_STATE, SSD_D_INNER), F32)],
        compiler_params=_params(("parallel", "arbitrary")),
        name="ssd_bwd",
    )(y1, c16, sb, dt_raw, zs, dtb, alog, rb, ng)


NA_KEYS = WIN_H * GRID_W


NA_ROWS_PER_STEP = 32


def _na_body(q_ref, k_ref, v_ref, bias_ref, o_ref, *, rows):
    scale = NA_HEAD_DIM ** -0.5

    def step(i, carry):
        qss, kss, scores, probs = [], [], [], []
        for j in range(NA_ROWS_PER_STEP):
            r = i * NA_ROWS_PER_STEP + j
            rs = jnp.clip(r - WIN_H // 2, 0, rows - WIN_H)
            dy0 = rs - r + (WIN_H - 1)
            qs = pl.ds(pl.multiple_of(r * GRID_W, GRID_W), GRID_W)
            ks = pl.ds(pl.multiple_of(rs * GRID_W, GRID_W), NA_KEYS)
            s = lax.dot_general(q_ref[0, qs, :], k_ref[0, ks, :], (((1,), (1,)), ((), ())),
                                preferred_element_type=F32)
            scores.append(s * scale + bias_ref[0, dy0])
            qss.append(qs)
            kss.append(ks)
        for s in scores:
            p = jnp.exp(s - jnp.max(s, axis=-1, keepdims=True))
            probs.append((p.astype(BF16), 1.0 / jnp.sum(p, axis=-1, keepdims=True)))
        for qs, ks, (p, inv) in zip(qss, kss, probs):
            o = jnp.dot(p, v_ref[0, ks, :], preferred_element_type=F32)
            o_ref[0, qs, :] = (o * inv).astype(o_ref.dtype)
        return carry

    lax.fori_loop(0, rows // NA_ROWS_PER_STEP, step, 0)


def _na_bias_table(rpb):
    qc = np.arange(GRID_W)[:, None]
    kc = np.arange(GRID_W)[None, :]
    wstart = np.clip(qc - WIN_W // 2, 0, GRID_W - WIN_W)
    ok = (kc >= wstart) & (kc < wstart + WIN_W)
    dx = np.clip(kc - qc + WIN_W - 1, 0, 2 * WIN_W - 2)
    per_dy = jnp.where(ok[None, None], rpb[:, :, dx].astype(F32), -jnp.inf)
    dy = np.arange(WIN_H)[:, None] + np.arange(WIN_H)[None, :]
    tab = per_dy[:, dy]
    return jnp.transpose(tab, (0, 1, 3, 2, 4)).reshape(NA_HEADS, WIN_H, GRID_W, NA_KEYS)


def _natten(qkv, bias_tab):
    bsz, seq, _ = qkv.shape
    rows = seq // GRID_W
    assert seq % GRID_W == 0 and rows >= WIN_H and rows % NA_ROWS_PER_STEP == 0
    blk = lambda off: pl.BlockSpec((1, seq, NA_HEAD_DIM), lambda b, h: (b, 0, off + h))
    return pl.pallas_call(
        functools.partial(_na_body, rows=rows),
        grid=(bsz, NA_HEADS),
        in_specs=[blk(0), blk(NA_HEADS), blk(2 * NA_HEADS),
                  pl.BlockSpec((1, WIN_H, GRID_W, NA_KEYS), lambda b, h: (h, 0, 0, 0))],
        out_specs=blk(0),
        out_shape=jax.ShapeDtypeStruct((bsz, seq, D_MODEL), BF16),
        compiler_params=_params(("parallel", "parallel")),
        name="natten",
    )(qkv, qkv, qkv, bias_tab)


def _row(v):
    return v.reshape(1, -1).astype(F32)


def _rg_layer(x, shape, p):
    bsz, seq = shape
    gate = _matmul(x, p["w_gate"], act="gelu", out_dtype=BF16, bn=2048, name="rg_in_gate")
    uc = _matmul_conv(x.reshape(bsz, seq, D_MODEL), p["w_u"], p["cw"], p["cb"], name="rg_in_u_conv")
    y = _rg_core(uc, gate.reshape(bsz, seq, D_MODEL), p)
    return y.reshape(bsz * seq, D_MODEL), p["w_out"]


def _ssd_layer(x, shape, p):
    bsz, seq = shape
    zs = _matmul(x, p["w_z"], act="silu", out_dtype=BF16, bn=2048, name="ssd_in_z").reshape(bsz, seq, SSD_D_INNER)
    xbc = _matmul_conv(x.reshape(bsz, seq, D_MODEL), p["w_xbc"], p["cw"], p["cb"], act="silu", name="ssd_in_xbc_conv")
    dt = _matmul(x, p["w_dt"], bn=2 * SSD_HEADS, name="ssd_in_dt").reshape(bsz, seq, 2 * SSD_HEADS)
    y = _ssd_core(xbc, dt, zs, p["dtb"], p["alog"], p["dskip_x"], p["rf"], p["rb"], p["ng"])
    return y.reshape(bsz * seq, SSD_D_INNER), p["w_out"]


def _na_layer(x, shape, p):
    bsz, seq = shape
    qkv = _matmul(x, p["w_qkv"], p["b_qkv"], out_dtype=BF16, bn=2048, name="na_qkv").reshape(bsz, seq, 3 * D_MODEL)
    o = _natten(qkv, p["bias_tab"])
    return o.reshape(bsz * seq, D_MODEL), p["w_out"]


def _head_expand_matrix(offset):
    r = np.zeros((2 * SSD_HEADS, SSD_D_INNER), np.float32)
    for h in range(SSD_HEADS):
        r[offset + h, h * SSD_HEAD_DIM:(h + 1) * SSD_HEAD_DIM] = 1.0
    return jnp.asarray(r, BF16)


def _rg_params(j, w_in16, conv_w, conv_b, w_a, b_a, w_x, b_x, lam, w_out16):
    return dict(w_gate=_Weight(w_in16, j, 0, D_MODEL), w_u=_Weight(w_in16, j, D_MODEL, D_MODEL),
                cw=conv_w[j], cb=_row(conv_b[j]),
                wg=jnp.concatenate([w_a[j], w_x[j]], axis=-1).astype(BF16),
                ba=b_a[j].reshape(2, 1, D_MODEL), bx=b_x[j].reshape(2, 1, D_MODEL),
                lam=lam[j].reshape(2, 1, D_MODEL), w_out=_Weight(w_out16, j, 0, D_MODEL))


def _ssd_params(j, w_in16, conv_w, conv_b, dt_bias, a_log, d_skip, norm_g, w_out16):
    return dict(w_z=_Weight(w_in16, j, 0, SSD_D_INNER),
                w_xbc=_Weight(w_in16, j, SSD_D_INNER, SSD_CONV_DIM),
                w_dt=_Weight(w_in16, j, SSD_D_INNER + SSD_CONV_DIM, 2 * SSD_HEADS),
                cw=conv_w[j], cb=_row(conv_b[j]), dtb=_row(dt_bias[j]), alog=_row(a_log[j]),
                dskip_x=_row(jnp.repeat(d_skip[j], SSD_HEAD_DIM)),
                rf=_head_expand_matrix(0), rb=_head_expand_matrix(SSD_HEADS),
                ng=_row(norm_g[j]), w_out=_Weight(w_out16, j, 0, D_MODEL))


def _na_params(j, w_qkv16, b_qkv, rpb, w_out16):
    return dict(w_qkv=_Weight(w_qkv16, j, 0, 3 * D_MODEL), b_qkv=_row(b_qkv[j]), bias_tab=_na_bias_table(rpb[j]),
                w_out=_Weight(w_out16, j, 0, D_MODEL))


def kernel(x_prompt, x_sample, rg_w_in, rg_conv_w, rg_conv_b, rg_w_a, rg_b_a, rg_w_x, rg_b_x, rg_lambda, rg_w_out,
           ssd_w_in, ssd_conv_w, ssd_conv_b, ssd_dt_bias, ssd_a_log, ssd_d, ssd_norm_g, ssd_w_out,
           na_w_qkv, na_b_qkv, na_rpb, na_w_out, mlp_w_up, mlp_w_down, ln1_g, ln1_b, ln2_g, ln2_b):
    cast = lambda w: w.astype(BF16)
    rg_w_in16, rg_w_out16, ssd_w_in16, ssd_w_out16 = cast(rg_w_in), cast(rg_w_out), cast(ssd_w_in), cast(ssd_w_out)
    na_w_qkv16, na_w_out16, mlp_w_up16, mlp_w_down16 = cast(na_w_qkv), cast(na_w_out), cast(mlp_w_up), cast(mlp_w_down)
    layers = []
    for i in range(DEPTH):
        kind, j = i % 3, i // 3
        if kind == 0:
            fn, p = _rg_layer, _rg_params(j, rg_w_in16, rg_conv_w, rg_conv_b, rg_w_a, rg_b_a, rg_w_x, rg_b_x,
                                          rg_lambda, rg_w_out16)
        elif kind == 1:
            fn, p = _ssd_layer, _ssd_params(j, ssd_w_in16, ssd_conv_w, ssd_conv_b, ssd_dt_bias, ssd_a_log, ssd_d,
                                            ssd_norm_g, ssd_w_out16)
        else:
            fn, p = _na_layer, _na_params(j, na_w_qkv16, na_b_qkv, na_rpb, na_w_out16)
        layers.append((fn, p, _Weight(mlp_w_up16, i, 0, MLP_HIDDEN), _Weight(mlp_w_down16, i, 0, D_MODEL),
                       _row(ln1_g[i]), _row(ln1_b[i]), _row(ln2_g[i]), _row(ln2_b[i])))

    def trunk(x):
        bsz, seq, _ = x.shape
        x = x.reshape(bsz * seq, D_MODEL)
        for fn, p, w_up, w_down, g1, b1, g2, b2 in layers:
            y, w_out = fn(x, (bsz, seq), p)
            x = _matmul_ln(y, w_out, x, g1, b1, name="mixer_out_ln")
            hid = _matmul(x, w_up, act="relu2", out_dtype=BF16, bn=2048, name="mlp_up")
            x = _matmul_ln(hid, w_down, x, g2, b2, name="mlp_down_ln")
        return x.reshape(bsz, seq, D_MODEL)

    return trunk(x_prompt), trunk(x_sample)
```

```python
import functools
import math
from typing import NamedTuple

import numpy as np
import jax
import jax.numpy as jnp
from jax import lax
from jax.experimental import pallas as pl
from jax.experimental.pallas import tpu as pltpu

F32 = jnp.float32
BF16 = jnp.bfloat16

D_MODEL = 2048
DEPTH = 4
ALPHA = (2 * DEPTH) ** 0.25
LN_EPS = 1e-5
CONV_W = 4
LRU_BLOCKS = 8
LRU_BW = D_MODEL // LRU_BLOCKS
LRU_C = 8.0
SSD_D_INNER = 2 * D_MODEL
SSD_HEAD_DIM = 64
SSD_HEADS = SSD_D_INNER // SSD_HEAD_DIM
SSD_GROUPS = 8
SSD_STATE = 128
SSD_CHUNK = 128
SSD_GROUP_W = SSD_D_INNER // SSD_GROUPS
SSD_CONV_DIM = SSD_D_INNER + 2 * SSD_GROUPS * SSD_STATE
GRID_W = 64
WIN_H = 8
WIN_W = 16
NA_HEAD_DIM = 128
NA_HEADS = D_MODEL // NA_HEAD_DIM
MLP_HIDDEN = 4 * D_MODEL

V7X_SUBLANES = 8
V7X_LANES = 128
V7X_VMEM_LIMIT_BYTES = 56 * 1024 * 1024


class _Weight(NamedTuple):
    stack: jax.Array
    layer: int
    col_off: int
    n: int

    def spec(self, rows, bn, index):
        assert self.col_off % bn == 0 and self.n % bn == 0
        off = self.col_off // bn
        return pl.BlockSpec((None, rows, bn), lambda *g: (self.layer, index(*g)[0], off + index(*g)[1]))

    def resident_spec(self):
        k = self.stack.shape[1]
        assert self.col_off % self.n == 0
        return pl.BlockSpec((None, k, self.n), lambda *g: (self.layer, 0, self.col_off // self.n),
                            pipeline_mode=pl.Buffered(1))


def _params(semantics, vmem_bytes=V7X_VMEM_LIMIT_BYTES):
    return pltpu.CompilerParams(dimension_semantics=semantics, vmem_limit_bytes=vmem_bytes)


def _sigmoid(x):
    return 1.0 / (1.0 + jnp.exp(-x))


def _softplus(x):
    return jnp.maximum(x, 0.0) + jnp.log1p(jnp.exp(-jnp.abs(x)))


def _gelu_tanh(x):
    c = math.sqrt(2.0 / math.pi)
    return 0.5 * x * (1.0 + jnp.tanh(c * (x + 0.044715 * (x * x * x))))


def _relu2(x):
    r = jnp.maximum(x, 0.0)
    return r * r


def _silu(x):
    return x * _sigmoid(x)


_ACTS = {"none": lambda v: v, "gelu": _gelu_tanh, "relu2": _relu2, "silu": _silu}


def _mm_body(*refs, act, has_bias):
    if has_bias:
        x_ref, w_ref, b_ref, o_ref = refs
    else:
        x_ref, w_ref, o_ref = refs
    acc = jnp.dot(x_ref[...].astype(BF16), w_ref[...], preferred_element_type=F32)
    if has_bias:
        acc = acc + b_ref[...]
    o_ref[...] = _ACTS[act](acc).astype(o_ref.dtype)


def _matmul(x, w, bias=None, *, act="none", out_dtype=F32, bm=1024, bn=1024, name="mm"):
    m, k = x.shape
    n = w.n
    assert k == w.stack.shape[1] and m % bm == 0, (x.shape, w.stack.shape, bm, bn)
    in_specs = [pl.BlockSpec((bm, k), lambda i, j: (i, 0)),
                w.spec(k, bn, lambda i, j: (0, j))]
    args = [x, w.stack]
    if bias is not None:
        in_specs.append(pl.BlockSpec((1, bn), lambda i, j: (0, j)))
        args.append(bias)
    return pl.pallas_call(
        functools.partial(_mm_body, act=act, has_bias=bias is not None),
        grid=(m // bm, n // bn),
        in_specs=in_specs,
        out_specs=pl.BlockSpec((bm, bn), lambda i, j: (i, j)),
        out_shape=jax.ShapeDtypeStruct((m, n), out_dtype),
        compiler_params=_params(("parallel", "parallel")),
        name=name,
    )(*args)


CONV_HALO = 16
CONV_SUBDOT = 256


def _mm_conv_body(x_ref, xp_ref, xn_ref, w_ref, cw_ref, cb_ref, o_ref, *, act, n_tiles):
    t = pl.program_id(1)
    rows = x_ref.shape[1]
    n = rows + 2 * CONV_HALO
    lo, hi = CONV_HALO, CONV_HALO + rows
    ext = jnp.concatenate([xp_ref[0] * (t > 0).astype(xp_ref.dtype), x_ref[0],
                           xn_ref[0] * (t < n_tiles - 1).astype(xn_ref.dtype)], axis=0).astype(BF16)
    for c in range(o_ref.shape[2] // CONV_SUBDOT):
        cols = slice(c * CONV_SUBDOT, (c + 1) * CONV_SUBDOT)
        y = jnp.dot(ext, w_ref[:, cols], preferred_element_type=F32)
        v = (cb_ref[:, cols] + pltpu.roll(y, 1, 0)[lo:hi] * cw_ref[0:1, cols] + y[lo:hi] * cw_ref[1:2, cols]
             + pltpu.roll(y, n - 1, 0)[lo:hi] * cw_ref[2:3, cols] + pltpu.roll(y, n - 2, 0)[lo:hi] * cw_ref[3:4, cols])
        o_ref[0, :, cols] = _ACTS[act](v).astype(o_ref.dtype)


def _matmul_conv(x, w, cw, cb, *, act="none", tl=1024, bn=1024, name="mm_conv"):
    bsz, seq, k = x.shape
    n = w.n
    assert seq % tl == 0 and bn % CONV_SUBDOT == 0 and tl % CONV_HALO == 0 and k == w.stack.shape[1]
    n_tiles = seq // tl
    hb = tl // CONV_HALO
    n_hblk = seq // CONV_HALO
    return pl.pallas_call(
        functools.partial(_mm_conv_body, act=act, n_tiles=n_tiles),
        grid=(bsz, n_tiles, n // bn),
        in_specs=[pl.BlockSpec((1, tl, k), lambda b, t, j: (b, t, 0)),
                  pl.BlockSpec((1, CONV_HALO, k), lambda b, t, j: (b, jnp.maximum(t * hb - 1, 0), 0)),
                  pl.BlockSpec((1, CONV_HALO, k), lambda b, t, j: (b, jnp.minimum((t + 1) * hb, n_hblk - 1), 0)),
                  w.spec(k, bn, lambda b, t, j: (0, j)),
                  pl.BlockSpec((CONV_W, bn), lambda b, t, j: (0, j)),
                  pl.BlockSpec((1, bn), lambda b, t, j: (0, j))],
        out_specs=pl.BlockSpec((1, tl, bn), lambda b, t, j: (b, t, j)),
        out_shape=jax.ShapeDtypeStruct((bsz, seq, n), F32),
        compiler_params=_params(("parallel", "parallel", "parallel")),
        name=name,
    )(x, x, x, w.stack, cw, cb)


MM_LN_TILES = {D_MODEL: (1024, 8), SSD_D_INNER: (512, 4), MLP_HIDDEN: (256, 2)}


def _mm_ln_body(x_ref, w_ref, res_ref, g_ref, b_ref, o_ref, *, split):
    hm = x_ref.shape[0] // split
    for h in range(split):
        rows = slice(h * hm, (h + 1) * hm)
        z = ALPHA * res_ref[rows, :] + jnp.dot(x_ref[rows, :], w_ref[...], preferred_element_type=F32)
        mu = jnp.mean(z, axis=-1, keepdims=True)
        zc = z - mu
        var = jnp.mean(zc * zc, axis=-1, keepdims=True)
        o_ref[rows, :] = zc * lax.rsqrt(var + LN_EPS) * g_ref[...] + b_ref[...]


def _matmul_ln(x, w, res, g, b, *, name="mm_ln"):
    m, k = x.shape
    n = w.n
    bm, split = MM_LN_TILES[k]
    assert k == w.stack.shape[1] and n == D_MODEL and m % bm == 0 and bm % split == 0
    return pl.pallas_call(
        functools.partial(_mm_ln_body, split=split),
        grid=(m // bm,),
        in_specs=[pl.BlockSpec((bm, k), lambda i: (i, 0)),
                  w.resident_spec(),
                  pl.BlockSpec((bm, n), lambda i: (i, 0)),
                  pl.BlockSpec((1, n), lambda i: (0, 0)),
                  pl.BlockSpec((1, n), lambda i: (0, 0))],
        out_specs=pl.BlockSpec((bm, n), lambda i: (i, 0)),
        out_shape=jax.ShapeDtypeStruct((m, n), F32),
        compiler_params=_params(("parallel",)),
        name=name,
    )(x, w.stack, res, g, b)


RG_TILE = 512
RG_CHUNKS = V7X_SUBLANES
RG_CHUNK_ROWS = RG_TILE // RG_CHUNKS
RG_SLABS = D_MODEL // V7X_LANES
LOG2E = math.log2(math.e)
SQRT_FLOOR = 1e-30


def _sigmoid_tanh(x):
    return 0.5 * jnp.tanh(0.5 * x) + 0.5


def _slab_rows(slab, chunk):
    return pl.ds(slab * RG_TILE + chunk, RG_CHUNK_ROWS, stride=RG_CHUNKS)


def _rg_gates(uc, wg_ref, ba_ref, bx_ref, lam_ref, a_s, b_s):
    nsp2 = (-LRU_C * LOG2E) * _softplus(-lam_ref[...])
    for n in range(LRU_BLOCKS):
        sl = slice(n * LRU_BW, (n + 1) * LRU_BW)
        un = uc[:, sl]
        gates = jnp.dot(un.astype(BF16), wg_ref[n], preferred_element_type=F32)
        r = _sigmoid_tanh(gates[:, :LRU_BW] + ba_ref[:, sl])
        i = _sigmoid_tanh(gates[:, LRU_BW:] + bx_ref[:, sl])
        a = jnp.exp2(r * nsp2[:, sl])
        v = 1.0 - a * a
        b = (v * lax.rsqrt(jnp.maximum(v, SQRT_FLOOR))) * (i * un)
        for half in range(LRU_BW // V7X_LANES):
            slab = n * (LRU_BW // V7X_LANES) + half
            lanes = slice(half * V7X_LANES, (half + 1) * V7X_LANES)
            for c in range(RG_CHUNKS):
                rows = slice(c * RG_CHUNK_ROWS, (c + 1) * RG_CHUNK_ROWS)
                a_s[_slab_rows(slab, c), :] = a[rows, lanes]
                b_s[_slab_rows(slab, c), :] = b[rows, lanes]


def _rg_scan_tile(a_s, b_s, carry, *, reverse):
    group = RG_SLABS * V7X_SUBLANES

    def step(i, hp):
        h, p = hp
        t = (RG_CHUNK_ROWS - 1 - i) if reverse else i
        rows = [pl.ds(pl.multiple_of(slab * RG_TILE + t * V7X_SUBLANES, V7X_SUBLANES), V7X_SUBLANES)
                for slab in range(RG_SLABS)]
        a = jnp.concatenate([a_s[r, :] for r in rows], axis=0)
        b = jnp.concatenate([b_s[r, :] for r in rows], axis=0)
        h = a * h + b
        p = p * a
        for slab, r in enumerate(rows):
            a_s[r, :] = h[slab * V7X_SUBLANES:(slab + 1) * V7X_SUBLANES]
            b_s[r, :] = p[slab * V7X_SUBLANES:(slab + 1) * V7X_SUBLANES]
        return h, p

    zeros = jnp.zeros((group, V7X_LANES), F32)
    q, p = lax.fori_loop(0, RG_CHUNK_ROWS, step, (zeros, zeros + 1.0))

    sub = lax.broadcasted_iota(jnp.int32, (group, V7X_LANES), 0) % V7X_SUBLANES
    for s in (1, 2, 4):
        shift = (group - s) if reverse else s
        m = (sub < V7X_SUBLANES - s) if reverse else (sub >= s)
        q = jnp.where(m, p * pltpu.roll(q, shift, 0) + q, q)
        p = jnp.where(m, p * pltpu.roll(p, shift, 0), p)
    c_in = carry[...]
    after = p * c_in + q
    first = (sub == V7X_SUBLANES - 1) if reverse else (sub == 0)
    entering = jnp.where(first, c_in, pltpu.roll(after, (group - 1) if reverse else 1, 0))
    last = 0 if reverse else V7X_SUBLANES - 1
    carry[...] = jnp.concatenate(
        [jnp.broadcast_to(after[slab * V7X_SUBLANES + last:slab * V7X_SUBLANES + last + 1, :],
                          (V7X_SUBLANES, V7X_LANES)) for slab in range(RG_SLABS)], axis=0)
    return entering


def _rg_body(*refs, reverse):
    if reverse:
        uc_ref, wg_ref, ba_ref, bx_ref, lam_ref, hf_ref, gate_ref, o_ref, a_s, b_s, carry = refs
    else:
        uc_ref, wg_ref, ba_ref, bx_ref, lam_ref, o_ref, a_s, b_s, carry = refs

    @pl.when(pl.program_id(1) == 0)
    def _():
        carry[...] = jnp.zeros_like(carry)

    _rg_gates(uc_ref[0], wg_ref, ba_ref, bx_ref, lam_ref, a_s, b_s)
    entering = _rg_scan_tile(a_s, b_s, carry, reverse=reverse)
    for slab in range(RG_SLABS):
        lanes = slice(slab * V7X_LANES, (slab + 1) * V7X_LANES)
        tile_rows = slice(slab * RG_TILE, (slab + 1) * RG_TILE)
        e = jnp.tile(entering[slab * V7X_SUBLANES:(slab + 1) * V7X_SUBLANES, :], (RG_CHUNK_ROWS, 1))
        h = a_s[tile_rows, :] + b_s[tile_rows, :] * e
        if not reverse:
            o_ref[0, :, lanes] = h
            continue
        a_s[tile_rows, :] = hf_ref[0, :, lanes] + h
        for c in range(RG_CHUNKS):
            rows = slice(c * RG_CHUNK_ROWS, (c + 1) * RG_CHUNK_ROWS)
            o_ref[0, rows, lanes] = (a_s[_slab_rows(slab, c), :]
                                     * gate_ref[0, rows, lanes].astype(F32)).astype(o_ref.dtype)


def _rg_core(uc, gate, p):
    bsz, seq, d = uc.shape
    tl = RG_TILE
    assert seq % tl == 0 and d == D_MODEL
    n_tiles = seq // tl
    row = pl.BlockSpec((1, d), lambda b, t: (0, 0))
    gate_w = pl.BlockSpec((LRU_BLOCKS, LRU_BW, 2 * LRU_BW), lambda b, t: (0, 0, 0))
    scan_scratch = [pltpu.VMEM((RG_SLABS * tl, V7X_LANES), F32), pltpu.VMEM((RG_SLABS * tl, V7X_LANES), F32),
                    pltpu.VMEM((RG_SLABS * V7X_SUBLANES, V7X_LANES), F32)]

    fwd = pl.BlockSpec((1, tl, d), lambda b, t: (b, t, 0))
    hf = pl.pallas_call(
        functools.partial(_rg_body, reverse=False),
        grid=(bsz, n_tiles),
        in_specs=[fwd, gate_w, row, row, row],
        out_specs=fwd,
        out_shape=jax.ShapeDtypeStruct((bsz, seq, d), F32),
        scratch_shapes=scan_scratch,
        compiler_params=_params(("parallel", "arbitrary")),
        name="rg_fwd",
    )(uc, p["wg"][0], p["ba"][0], p["bx"][0], p["lam"][0])

    bwd = pl.BlockSpec((1, tl, d), lambda b, t: (b, n_tiles - 1 - t, 0))
    return pl.pallas_call(
        functools.partial(_rg_body, reverse=True),
        grid=(bsz, n_tiles),
        in_specs=[bwd, gate_w, row, row, row, bwd, bwd],
        out_specs=bwd,
        out_shape=jax.ShapeDtypeStruct((bsz, seq, d), BF16),
        scratch_shapes=scan_scratch,
        compiler_params=_params(("parallel", "arbitrary")),
        name="rg_bwd",
    )(uc, p["wg"][1], p["ba"][1], p["bx"][1], p["lam"][1], hf, gate)


def _prefix_rows(x, rid):
    q = x.shape[0]
    s = 1
    while s < q:
        x = x + jnp.where(rid >= s, pltpu.roll(x, s, 0), 0.0)
        s *= 2
    return x


def _suffix_rows(x, rid):
    q = x.shape[0]
    s = 1
    while s < q:
        x = x + jnp.where(rid < q - s, pltpu.roll(x, q - s, 0), 0.0)
        s *= 2
    return x


def _expand_heads(v, r_ref, *, split=True):
    hi = v.astype(BF16)
    out = jnp.dot(hi, r_ref[...], preferred_element_type=F32)
    if split:
        out = out + jnp.dot((v - hi.astype(F32)).astype(BF16), r_ref[...], preferred_element_type=F32)
    return out


def _ssd_dt(dt_ref, dtb_ref, alog_ref):
    dt = _softplus(dt_ref[0] + dtb_ref[...])
    return dt, dt * (-jnp.exp(alog_ref[...]))


def _ssd1_body(x_ref, dt_ref, dtb_ref, alog_ref, dsk_ref, rf_ref, rb_ref, y1_ref, c_ref, sb_ref, hf_s):
    q = SSD_CHUNK

    @pl.when(pl.program_id(1) == 0)
    def _():
        hf_s[...] = jnp.zeros_like(hf_s)

    act = x_ref[0]
    xs = act[:, :SSD_D_INNER]
    bm = act[:, SSD_D_INNER:SSD_D_INNER + SSD_GROUPS * SSD_STATE]
    cm = act[:, SSD_D_INNER + SSD_GROUPS * SSD_STATE:]
    c_ref[0] = cm.astype(BF16)

    dt, dta = _ssd_dt(dt_ref, dtb_ref, alog_ref)
    rid = lax.broadcasted_iota(jnp.int32, (q, q), 0)
    cid = lax.broadcasted_iota(jnp.int32, (q, q), 1)
    is_fwd = cid < SSD_HEADS
    cum = jnp.where(is_fwd, _prefix_rows(dta, rid), _suffix_rows(dta, rid))
    tot = jnp.where(is_fwd[0:1, :], cum[q - 1:q, :], cum[0:1, :])
    e_in = jnp.exp(cum)
    w_end = dt * jnp.exp(tot - cum)

    ef_x = _expand_heads(e_in, rf_ref)
    xf = (xs * _expand_heads(w_end, rf_ref, split=False)).astype(BF16)
    xb = (xs * _expand_heads(w_end, rb_ref, split=False)).astype(BF16)

    dst = cum * LOG2E
    src_t = (dst - jnp.log2(dt)).T
    lower = rid >= cid
    upper = rid <= cid
    neg_inf = jnp.float32(-jnp.inf)
    lane_lo = cid < SSD_HEAD_DIM

    for g in range(SSD_GROUPS):
        ns = slice(g * SSD_STATE, (g + 1) * SSD_STATE)
        gs = slice(g * SSD_GROUP_W, (g + 1) * SSD_GROUP_W)
        bg = bm[:, ns]
        cg16 = cm[:, ns].astype(BF16)
        cb = lax.dot_general(cg16, bg.astype(BF16), (((1,), (1,)), ((), ())), preferred_element_type=F32)
        bgt16 = bg.T.astype(BF16)
        sf_t = jnp.dot(bgt16, xf[:, gs], preferred_element_type=F32)
        sb_ref[0, 0, :, gs] = jnp.dot(bgt16, xb[:, gs], preferred_element_type=F32).astype(sb_ref.dtype)
        h_in = hf_s[:, gs]
        y_off = jnp.dot(cg16, h_in.astype(BF16), preferred_element_type=F32) * ef_x[:, gs]
        hf_s[:, gs] = h_in * ef_x[q - 1:q, gs] + sf_t

        for pr in range(SSD_GROUP_W // (2 * SSD_HEAD_DIM)):
            h0 = g * (SSD_HEADS // SSD_GROUPS) + 2 * pr
            ms = []
            for h in (h0, h0 + 1):
                hb = SSD_HEADS + h
                d_f = dst[:, h:h + 1] - src_t[h:h + 1, :]
                d_b = dst[:, hb:hb + 1] - src_t[hb:hb + 1, :]
                m_f = jnp.exp2(jnp.where(lower, d_f, neg_inf))
                m_b = jnp.exp2(jnp.where(upper, d_b, neg_inf))
                ms.append((cb * (m_f + m_b)).astype(BF16))
            lhs = jnp.concatenate(ms, axis=1)
            ps = slice(h0 * SSD_HEAD_DIM, (h0 + 2) * SSD_HEAD_DIM)
            xp = xs[:, ps]
            rhs = jnp.concatenate([jnp.where(lane_lo, xp, 0.0).astype(BF16),
                                   jnp.where(lane_lo, 0.0, xp).astype(BF16)], axis=0)
            y_diag = jnp.dot(lhs, rhs, preferred_element_type=F32)
            os_ = slice(pr * 2 * SSD_HEAD_DIM, (pr + 1) * 2 * SSD_HEAD_DIM)
            y1_ref[0, :, ps] = (y_diag + y_off[:, os_] + xs[:, ps] * dsk_ref[:, ps]).astype(y1_ref.dtype)


SSD_BWD_CHUNKS = 2


def _ssd2_body(y1_ref, c_ref, sb_ref, dt_ref, zs_ref, dtb_ref, alog_ref, rb_ref, ng_ref, o_ref, hb_s):
    q = SSD_CHUNK

    @pl.when(pl.program_id(1) == 0)
    def _():
        hb_s[...] = jnp.zeros_like(hb_s)

    rid = lax.broadcasted_iota(jnp.int32, (q, q), 0)
    for sub in reversed(range(SSD_BWD_CHUNKS)):
        rows = slice(sub * q, (sub + 1) * q)
        dta = _softplus(dt_ref[0, rows, :] + dtb_ref[...]) * (-jnp.exp(alog_ref[...]))
        eb_x = _expand_heads(jnp.exp(_suffix_rows(dta, rid)), rb_ref)
        for g in range(SSD_GROUPS):
            ns = slice(g * SSD_STATE, (g + 1) * SSD_STATE)
            gs = slice(g * SSD_GROUP_W, (g + 1) * SSD_GROUP_W)
            h_in = hb_s[:, gs]
            y_off = jnp.dot(c_ref[0, rows, ns], h_in.astype(BF16), preferred_element_type=F32) * eb_x[:, gs]
            hb_s[:, gs] = h_in * eb_x[0:1, gs] + sb_ref[0, sub, :, gs].astype(F32)
            y = (y1_ref[0, rows, gs].astype(F32) + y_off) * zs_ref[0, rows, gs].astype(F32)
            ms = jnp.mean(y * y, axis=-1, keepdims=True)
            o_ref[0, rows, gs] = (y * lax.rsqrt(ms + LN_EPS) * ng_ref[:, gs]).astype(BF16)


def _ssd_core(xbc_act, dt_raw, zs, dtb, alog, dskip_x, rf, rb, ng):
    bsz, seq, _ = xbc_act.shape
    q = SSD_CHUNK
    assert seq % q == 0 and q == 2 * SSD_HEADS
    nc = seq // q
    row = lambda r, w: pl.BlockSpec((r, w), lambda b, c: (0, 0))
    y1, c16, sb = pl.pallas_call(
        _ssd1_body,
        grid=(bsz, nc),
        in_specs=[
            pl.BlockSpec((1, q, SSD_CONV_DIM), lambda b, c: (b, c, 0)),
            pl.BlockSpec((1, q, 2 * SSD_HEADS), lambda b, c: (b, c, 0)),
            row(1, 2 * SSD_HEADS), row(1, 2 * SSD_HEADS),
            row(1, SSD_D_INNER), row(2 * SSD_HEADS, SSD_D_INNER), row(2 * SSD_HEADS, SSD_D_INNER),
        ],
        out_specs=[
            pl.BlockSpec((1, q, SSD_D_INNER), lambda b, c: (b, c, 0)),
            pl.BlockSpec((1, q, SSD_GROUPS * SSD_STATE), lambda b, c: (b, c, 0)),
            pl.BlockSpec((1, 1, SSD_STATE, SSD_D_INNER), lambda b, c: (b, c, 0, 0)),
        ],
        out_shape=[
            jax.ShapeDtypeStruct((bsz, seq, SSD_D_INNER), BF16),
            jax.ShapeDtypeStruct((bsz, seq, SSD_GROUPS * SSD_STATE), BF16),
            jax.ShapeDtypeStruct((bsz, nc, SSD_STATE, SSD_D_INNER), BF16),
        ],
        scratch_shapes=[pltpu.VMEM((SSD_STATE, SSD_D_INNER), F32)],
        compiler_params=_params(("parallel", "arbitrary")),
        name="ssd_fwd",
    )(xbc_act, dt_raw, dtb, alog, dskip_x, rf, rb)

    assert nc % SSD_BWD_CHUNKS == 0
    ns, qs = nc // SSD_BWD_CHUNKS, q * SSD_BWD_CHUNKS
    rev = lambda b, c: (b, ns - 1 - c, 0)
    return pl.pallas_call(
        _ssd2_body,
        grid=(bsz, ns),
        in_specs=[
            pl.BlockSpec((1, qs, SSD_D_INNER), rev),
            pl.BlockSpec((1, qs, SSD_GROUPS * SSD_STATE), rev),
            pl.BlockSpec((1, SSD_BWD_CHUNKS, SSD_STATE, SSD_D_INNER), lambda b, c: (b, ns - 1 - c, 0, 0)),
            pl.BlockSpec((1, qs, 2 * SSD_HEADS), rev),
            pl.BlockSpec((1, qs, SSD_D_INNER), rev),
            row(1, 2 * SSD_HEADS), row(1, 2 * SSD_HEADS), row(2 * SSD_HEADS, SSD_D_INNER), row(1, SSD_D_INNER),
        ],
        out_specs=pl.BlockSpec((1, qs, SSD_D_INNER), rev),
        out_shape=jax.ShapeDtypeStruct((bsz, seq, SSD_D_INNER), BF16),
        scratch_shapes=[pltpu.VMEM((SSD_STATE, SSD_D_INNER), F32)],
        compiler_params=_params(("parallel", "arbitrary")),
        name="ssd_bwd",
    )(y1, c16, sb, dt_raw, zs, dtb, alog, rb, ng)


NA_KEYS = WIN_H * GRID_W


NA_ROWS_PER_STEP = 32


def _na_body(q_ref, k_ref, v_ref, bias_ref, o_ref, *, rows):
    scale = NA_HEAD_DIM ** -0.5

    def step(i, carry):
        qss, kss, scores, probs = [], [], [], []
        for j in range(NA_ROWS_PER_STEP):
            r = i * NA_ROWS_PER_STEP + j
            rs = jnp.clip(r - WIN_H // 2, 0, rows - WIN_H)
            dy0 = rs - r + (WIN_H - 1)
            qs = pl.ds(pl.multiple_of(r * GRID_W, GRID_W), GRID_W)
            ks = pl.ds(pl.multiple_of(rs * GRID_W, GRID_W), NA_KEYS)
            s = lax.dot_general(q_ref[0, qs, :], k_ref[0, ks, :], (((1,), (1,)), ((), ())),
                                preferred_element_type=F32)
            scores.append(s * scale + bias_ref[0, dy0])
            qss.append(qs)
            kss.append(ks)
        for s in scores:
            p = jnp.exp(s - jnp.max(s, axis=-1, keepdims=True))
            probs.append((p.astype(BF16), 1.0 / jnp.sum(p, axis=-1, keepdims=True)))
        for qs, ks, (p, inv) in zip(qss, kss, probs):
            o = jnp.dot(p, v_ref[0, ks, :], preferred_element_type=F32)
            o_ref[0, qs, :] = (o * inv).astype(o_ref.dtype)
        return carry

    lax.fori_loop(0, rows // NA_ROWS_PER_STEP, step, 0)


def _na_bias_table(rpb):
    qc = np.arange(GRID_W)[:, None]
    kc = np.arange(GRID_W)[None, :]
    wstart = np.clip(qc - WIN_W // 2, 0, GRID_W - WIN_W)
    ok = (kc >= wstart) & (kc < wstart + WIN_W)
    dx = np.clip(kc - qc + WIN_W - 1, 0, 2 * WIN_W - 2)
    per_dy = jnp.where(ok[None, None], rpb[:, :, dx].astype(F32), -jnp.inf)
    dy = np.arange(WIN_H)[:, None] + np.arange(WIN_H)[None, :]
    tab = per_dy[:, dy]
    return jnp.transpose(tab, (0, 1, 3, 2, 4)).reshape(NA_HEADS, WIN_H, GRID_W, NA_KEYS)


def _natten(qkv, bias_tab):
    bsz, seq, _ = qkv.shape
    rows = seq // GRID_W
    assert seq % GRID_W == 0 and rows >= WIN_H and rows % NA_ROWS_PER_STEP == 0
    blk = lambda off: pl.BlockSpec((1, seq, NA_HEAD_DIM), lambda b, h: (b, 0, off + h))
    return pl.pallas_call(
        functools.partial(_na_body, rows=rows),
        grid=(bsz, NA_HEADS),
        in_specs=[blk(0), blk(NA_HEADS), blk(2 * NA_HEADS),
                  pl.BlockSpec((1, WIN_H, GRID_W, NA_KEYS), lambda b, h: (h, 0, 0, 0))],
        out_specs=blk(0),
        out_shape=jax.ShapeDtypeStruct((bsz, seq, D_MODEL), BF16),
        compiler_params=_params(("parallel", "parallel")),
        name="natten",
    )(qkv, qkv, qkv, bias_tab)


def _row(v):
    return v.reshape(1, -1).astype(F32)


def _rg_layer(x, shape, p):
    bsz, seq = shape
    gate = _matmul(x, p["w_gate"], act="gelu", out_dtype=BF16, bn=2048, name="rg_in_gate")
    uc = _matmul_conv(x.reshape(bsz, seq, D_MODEL), p["w_u"], p["cw"], p["cb"], name="rg_in_u_conv")
    y = _rg_core(uc, gate.reshape(bsz, seq, D_MODEL), p)
    return y.reshape(bsz * seq, D_MODEL), p["w_out"]


def _ssd_layer(x, shape, p):
    bsz, seq = shape
    zs = _matmul(x, p["w_z"], act="silu", out_dtype=BF16, bn=2048, name="ssd_in_z").reshape(bsz, seq, SSD_D_INNER)
    xbc = _matmul_conv(x.reshape(bsz, seq, D_MODEL), p["w_xbc"], p["cw"], p["cb"], act="silu", name="ssd_in_xbc_conv")
    dt = _matmul(x, p["w_dt"], bn=2 * SSD_HEADS, name="ssd_in_dt").reshape(bsz, seq, 2 * SSD_HEADS)
    y = _ssd_core(xbc, dt, zs, p["dtb"], p["alog"], p["dskip_x"], p["rf"], p["rb"], p["ng"])
    return y.reshape(bsz * seq, SSD_D_INNER), p["w_out"]


def _na_layer(x, shape, p):
    bsz, seq = shape
    qkv = _matmul(x, p["w_qkv"], p["b_qkv"], out_dtype=BF16, bn=2048, name="na_qkv").reshape(bsz, seq, 3 * D_MODEL)
    o = _natten(qkv, p["bias_tab"])
    return o.reshape(bsz * seq, D_MODEL), p["w_out"]


def _head_expand_matrix(offset):
    r = np.zeros((2 * SSD_HEADS, SSD_D_INNER), np.float32)
    for h in range(SSD_HEADS):
        r[offset + h, h * SSD_HEAD_DIM:(h + 1) * SSD_HEAD_DIM] = 1.0
    return jnp.asarray(r, BF16)


def _rg_params(j, w_in16, conv_w, conv_b, w_a, b_a, w_x, b_x, lam, w_out16):
    return dict(w_gate=_Weight(w_in16, j, 0, D_MODEL), w_u=_Weight(w_in16, j, D_MODEL, D_MODEL),
                cw=conv_w[j], cb=_row(conv_b[j]),
                wg=jnp.concatenate([w_a[j], w_x[j]], axis=-1).astype(BF16),
                ba=b_a[j].reshape(2, 1, D_MODEL), bx=b_x[j].reshape(2, 1, D_MODEL),
                lam=lam[j].reshape(2, 1, D_MODEL), w_out=_Weight(w_out16, j, 0, D_MODEL))


def _ssd_params(j, w_in16, conv_w, conv_b, dt_bias, a_log, d_skip, norm_g, w_out16):
    return dict(w_z=_Weight(w_in16, j, 0, SSD_D_INNER),
                w_xbc=_Weight(w_in16, j, SSD_D_INNER, SSD_CONV_DIM),
                w_dt=_Weight(w_in16, j, SSD_D_INNER + SSD_CONV_DIM, 2 * SSD_HEADS),
                cw=conv_w[j], cb=_row(conv_b[j]), dtb=_row(dt_bias[j]), alog=_row(a_log[j]),
                dskip_x=_row(jnp.repeat(d_skip[j], SSD_HEAD_DIM)),
                rf=_head_expand_matrix(0), rb=_head_expand_matrix(SSD_HEADS),
                ng=_row(norm_g[j]), w_out=_Weight(w_out16, j, 0, D_MODEL))


def _na_params(j, w_qkv16, b_qkv, rpb, w_out16):
    return dict(w_qkv=_Weight(w_qkv16, j, 0, 3 * D_MODEL), b_qkv=_row(b_qkv[j]), bias_tab=_na_bias_table(rpb[j]),
                w_out=_Weight(w_out16, j, 0, D_MODEL))


def kernel(x_prompt, x_sample, rg_w_in, rg_conv_w, rg_conv_b, rg_w_a, rg_b_a, rg_w_x, rg_b_x, rg_lambda, rg_w_out,
           ssd_w_in, ssd_conv_w, ssd_conv_b, ssd_dt_bias, ssd_a_log, ssd_d, ssd_norm_g, ssd_w_out,
           na_w_qkv, na_b_qkv, na_rpb, na_w_out, mlp_w_up, mlp_w_down, ln1_g, ln1_b, ln2_g, ln2_b):
    cast = lambda w: w.astype(BF16)
    rg_w_in16, rg_w_out16, ssd_w_in16, ssd_w_out16 = cast(rg_w_in), cast(rg_w_out), cast(ssd_w_in), cast(ssd_w_out)
    na_w_qkv16, na_w_out16, mlp_w_up16, mlp_w_down16 = cast(na_w_qkv), cast(na_w_out), cast(mlp_w_up), cast(mlp_w_down)
    layers = []
    for i in range(DEPTH):
        kind, j = i % 3, i // 3
        if kind == 0:
            fn, p = _rg_layer, _rg_params(j, rg_w_in16, rg_conv_w, rg_conv_b, rg_w_a, rg_b_a, rg_w_x, rg_b_x,
                                          rg_lambda, rg_w_out16)
        elif kind == 1:
            fn, p = _ssd_layer, _ssd_params(j, ssd_w_in16, ssd_conv_w, ssd_conv_b, ssd_dt_bias, ssd_a_log, ssd_d,
                                            ssd_norm_g, ssd_w_out16)
        else:
            fn, p = _na_layer, _na_params(j, na_w_qkv16, na_b_qkv, na_rpb, na_w_out16)
        layers.append((fn, p, _Weight(mlp_w_up16, i, 0, MLP_HIDDEN), _Weight(mlp_w_down16, i, 0, D_MODEL),
                       _row(ln1_g[i]), _row(ln1_b[i]), _row(ln2_g[i]), _row(ln2_b[i])))

    def trunk(x):
        bsz, seq, _ = x.shape
        x = x.reshape(bsz * seq, D_MODEL)
        for fn, p, w_up, w_down, g1, b1, g2, b2 in layers:
            y, w_out = fn(x, (bsz, seq), p)
            x = _matmul_ln(y, w_out, x, g1, b1, name="mixer_out_ln")
            hid = _matmul(x, w_up, act="relu2", out_dtype=BF16, bn=2048, name="mlp_up")
            x = _matmul_ln(hid, w_down, x, g2, b2, name="mlp_down_ln")
        return x.reshape(bsz, seq, D_MODEL)

    return trunk(x_prompt), trunk(x_sample)
```
